```python
import functools
import jax, jax.numpy as jnp
from jax import lax
import numpy as np

D_MODEL = 4096
BATCH = 1
SEQ = 8192
DEPTH = 1
DEC_BATCH = 32
DEC_SEQ = 1
PAST_LEN = 8192
PAGE_SIZE = 128

RW_HEADS = 32
RW_HEAD_DIM = 64
RW_WIDTH = RW_HEADS * RW_HEAD_DIM
DECAY_LORA = 96
ICLR_LORA = 96
GATE_LORA = 256
LNX_EPS = 64e-5
RW_PROJ = 3 * RW_WIDTH + DECAY_LORA + ICLR_LORA + GATE_LORA

MB_HEADS = 16
MB_HEAD_DIM = 128
MB_WIDTH = MB_HEADS * MB_HEAD_DIM
MB_BLOCK = 256
MB_TOPK = 3
MB_QCHUNK = 32

D_FF = ((8 * D_MODEL + 3 * 256 - 1) // (3 * 256)) * 256
RMS_EPS = 1e-6

IN_COLS = 3 * MB_WIDTH + RW_PROJ + 2 * D_MODEL

kernel_name = 'hybrid_rwkv7_moba_step'

F32 = jnp.float32


def _rmsnorm(x, g):
    xf = x.astype(F32)
    y = xf * lax.rsqrt(jnp.mean(xf * xf, axis=-1, keepdims=True) + RMS_EPS)
    return (y * g.astype(F32)).astype(x.dtype)


def _rwkv_scan(r, w, k, v, a, b, s0):
    def step(s, inp):
        r_t, w_t, k_t, v_t, a_t, b_t = inp
        sa = jnp.einsum('bhvk,bhk->bhv', s, a_t)
        s = s * w_t[:, :, None, :] + sa[..., None] * b_t[:, :, None, :] + v_t[..., None] * k_t[:, :, None, :]
        y = jnp.einsum('bhvk,bhk->bhv', s, r_t)
        return s, y
    xs = tuple(jnp.moveaxis(t, 1, 0) for t in (r, w, k, v, a, b))
    s, ys = lax.scan(step, s0, xs)
    return jnp.moveaxis(ys, 0, 1), s


def _rwkv_branch(p, shift_prev, s0, mu, w0, w2, a0, a2, g2, k_k, k_a, r_k, lnx_g, lnx_b):
    B, T, _ = p.shape
    prev = jnp.concatenate([shift_prev[:, None, :].astype(p.dtype), p[:, :-1]], axis=1)
    pm = p + (prev - p) * mu
    r, k, v, wd, ad, gd = jnp.split(pm, [RW_WIDTH, 2 * RW_WIDTH, 3 * RW_WIDTH,
                                         3 * RW_WIDTH + DECAY_LORA,
                                         3 * RW_WIDTH + DECAY_LORA + ICLR_LORA], axis=-1)
    w_log = -jax.nn.softplus(-(w0 + jnp.tanh(wd) @ w2).astype(F32)) - 0.5
    decay = jnp.exp(-jnp.exp(w_log))
    a_lr = jax.nn.sigmoid((a0 + ad @ a2).astype(F32))
    g = jax.nn.sigmoid(gd) @ g2
    hs = lambda t: t.reshape(B, T, RW_HEADS, RW_HEAD_DIM)
    kf = k.astype(F32)
    kk = hs(kf * k_k.astype(F32))
    kk = kk / jnp.maximum(jnp.linalg.norm(kk, axis=-1, keepdims=True), 1e-12)
    kf = kf * (1.0 + (a_lr - 1.0) * k_a.astype(F32))
    rh, kh, vh = hs(r.astype(F32)), hs(kf), hs(v.astype(F32))
    y, s = _rwkv_scan(rh, hs(decay), kh, vh, -kk, kk * hs(a_lr), s0.astype(F32))
    mean = jnp.mean(y, axis=-1, keepdims=True)
    var = jnp.mean(jnp.square(y - mean), axis=-1, keepdims=True)
    gn_g = lnx_g.astype(F32).reshape(RW_HEADS, RW_HEAD_DIM)
    gn_b = lnx_b.astype(F32).reshape(RW_HEADS, RW_HEAD_DIM)
    y = (y - mean) * lax.rsqrt(var + LNX_EPS) * gn_g + gn_b
    y = y + jnp.sum(rh * kh * r_k.astype(F32), axis=-1, keepdims=True) * vh
    y = y.reshape(B, T, RW_WIDTH) * g.astype(F32)
    return y.astype(p.dtype), p[:, -1], s.astype(s0.dtype)


def _moba_core(q, qpos, kmean, gather):
    B, Q, H, D = q.shape
    nb = kmean.shape[1]
    n_top = min(MB_TOPK, nb)
    cur = qpos // MB_BLOCK
    gs = jnp.einsum('bqhd,bnhd->bqhn', q.astype(F32), kmean.astype(F32))
    past = jnp.arange(nb)[None, :] < cur[:, None]
    gs = jnp.where(past[None, :, None, :], gs, -jnp.inf)
    _, top = lax.top_k(gs, n_top)
    sel_ok = jnp.arange(n_top)[None, :] < jnp.minimum(cur, MB_TOPK)[:, None]
    blocks = jnp.concatenate([top.astype(jnp.int32),
                              jnp.broadcast_to(cur[None, :, None, None], (B, Q, H, 1)).astype(jnp.int32)], axis=-1)
    ok = jnp.concatenate([sel_ok, jnp.ones((Q, 1), dtype=bool)], axis=-1)
    kb, vb = gather(blocks)
    kpos = blocks[..., None] * MB_BLOCK + jnp.arange(MB_BLOCK, dtype=jnp.int32)
    mask = ok[None, :, None, :, None] & (kpos <= qpos[None, :, None, None, None])
    s = jnp.einsum('bqhd,bqhnsd->bqhns', q, kb).astype(F32) * (MB_HEAD_DIM ** -0.5)
    s = jnp.where(mask, s, -jnp.inf)
    p = jax.nn.softmax(s.reshape(B, Q, H, -1), axis=-1).reshape(s.shape)
    return jnp.einsum('bqhns,bqhnsd->bqhd', p.astype(vb.dtype), vb)


def _moba_prompt(q, k, v):
    B, T, H, D = q.shape
    nb = -(-T // MB_BLOCK)
    pad = nb * MB_BLOCK - T
    kp = jnp.pad(k, ((0, 0), (0, pad), (0, 0), (0, 0))).reshape(B, nb, MB_BLOCK, H, D)
    vp = jnp.pad(v, ((0, 0), (0, pad), (0, 0), (0, 0))).reshape(B, nb, MB_BLOCK, H, D)
    kmean = jnp.mean(kp.astype(F32), axis=2)
    kh = kp.transpose(0, 3, 1, 2, 4)
    vh = vp.transpose(0, 3, 1, 2, 4)
    bi = jnp.arange(B)[:, None, None, None]
    hi = jnp.arange(H)[None, None, :, None]

    def gather(blocks):
        return kh[bi, hi, blocks], vh[bi, hi, blocks]

    nq = T // MB_QCHUNK
    qc = q.reshape(B, nq, MB_QCHUNK, H, D).transpose(1, 0, 2, 3, 4)
    pos = jnp.arange(T, dtype=jnp.int32).reshape(nq, MB_QCHUNK)
    out = lax.map(lambda a: _moba_core(a[0], a[1], kmean, gather), (qc, pos))
    return out.transpose(1, 0, 2, 3, 4).reshape(B, T, H * D)


def _moba_sample(q, k, v, cache_k, cache_v, page_table):
    B, T, H, D = q.shape
    n_past_pages = page_table.shape[1]
    past_len = n_past_pages * PAGE_SIZE
    n_new_pages = -(-T // PAGE_SIZE)
    total = past_len + T
    nb = -(-total // MB_BLOCK)
    ppb = MB_BLOCK // PAGE_SIZE
    pad = n_new_pages * PAGE_SIZE - T
    kn = jnp.pad(k, ((0, 0), (0, pad), (0, 0), (0, 0))).reshape(B, n_new_pages, PAGE_SIZE, H, D)
    vn = jnp.pad(v, ((0, 0), (0, pad), (0, 0), (0, 0))).reshape(B, n_new_pages, PAGE_SIZE, H, D)
    page_sums = jnp.concatenate([jnp.sum(cache_k[page_table], axis=2, dtype=F32),
                                 jnp.sum(kn, axis=2, dtype=F32)], axis=1)
    page_sums = jnp.pad(page_sums, ((0, 0), (0, nb * ppb - page_sums.shape[1]), (0, 0), (0, 0)))
    kmean = jnp.sum(page_sums.reshape(B, nb, ppb, H, D), axis=2) / MB_BLOCK
    bi = jnp.arange(B)[:, None, None, None, None]
    hi = jnp.arange(H)[None, None, :, None, None]

    def gather(blocks):
        lp = blocks[..., None] * ppb + jnp.arange(ppb, dtype=jnp.int32)
        is_past = lp < n_past_pages
        phys = page_table[bi, jnp.clip(lp, 0, n_past_pages - 1)]
        npg = jnp.clip(lp - n_past_pages, 0, n_new_pages - 1)

        def rows(pool, new):
            old = pool[phys, :, hi, :]
            fresh = new[bi, npg, :, hi, :]
            r = jnp.where(is_past[..., None, None], old, fresh)
            return r.reshape(r.shape[:4] + (MB_BLOCK, D))

        return rows(cache_k, kn), rows(cache_v, vn)

    qpos = past_len + jnp.arange(T, dtype=jnp.int32)
    return _moba_core(q, qpos, kmean, gather).reshape(B, T, H * D)


def _layer(x, shift_prev, wkv0, moba_fn, norm_mix, w_in, rwkv_mu, w0, w2, a0, a2, g2, k_k, k_a, r_k,
           lnx_g, lnx_b, up_rw, up_mb, w_o, norm_ffn, w_gate, w_up, w_down):
    B, T, _ = x.shape
    xn = _rmsnorm(x, norm_mix)
    proj = xn @ w_in
    q, k, v, p_rw, g_rw, g_mb = jnp.split(proj, [MB_WIDTH, 2 * MB_WIDTH, 3 * MB_WIDTH,
                                                 3 * MB_WIDTH + RW_PROJ,
                                                 3 * MB_WIDTH + RW_PROJ + D_MODEL], axis=-1)
    heads = lambda t: t.reshape(B, T, MB_HEADS, MB_HEAD_DIM)
    q, k, v = heads(q), heads(k), heads(v)
    y_rw, shift_new, wkv_new = _rwkv_branch(p_rw, shift_prev, wkv0, rwkv_mu, w0, w2, a0, a2, g2,
                                            k_k, k_a, r_k, lnx_g, lnx_b)
    y_mb = moba_fn(q, k, v)
    merged = jax.nn.sigmoid(g_rw) * (y_rw @ up_rw) + jax.nn.sigmoid(g_mb) * (y_mb @ up_mb)
    x = x + merged @ w_o
    xn = _rmsnorm(x, norm_ffn)
    x = x + (jax.nn.silu(xn @ w_gate) * (xn @ w_up)) @ w_down
    return x, (k, v, wkv_new, shift_new)


def setup_inputs(seed: int = 0) -> dict:
    key = jax.random.key(seed)
    ks = iter(jax.random.split(key, 32))
    nrm = lambda shape, scale: jax.random.normal(next(ks), shape, F32) * scale
    L = DEPTH
    n_pages = PAST_LEN // PAGE_SIZE
    n_used = DEC_BATCH * n_pages
    n_pool = n_used + -(-n_used // 4)
    page_table = jax.random.permutation(next(ks), n_pool)[:n_used].reshape(DEC_BATCH, n_pages).astype(jnp.int32)
    return {
        'x_prompt': nrm((BATCH, SEQ, D_MODEL), 1.0),
        'x_sample': nrm((DEC_BATCH, DEC_SEQ, D_MODEL), 1.0),
        'cache_k': nrm((L, n_pool, PAGE_SIZE, MB_HEADS, MB_HEAD_DIM), 1.0),
        'cache_v': nrm((L, n_pool, PAGE_SIZE, MB_HEADS, MB_HEAD_DIM), 1.0),
        'state_wkv': nrm((L, DEC_BATCH, RW_HEADS, RW_HEAD_DIM, RW_HEAD_DIM), 0.3),
        'state_shift': nrm((L, DEC_BATCH, RW_PROJ), 1.0),
        'page_table': page_table,
        'norm_mix': 1.0 + nrm((L, D_MODEL), 0.01),
        'w_in': nrm((L, D_MODEL, IN_COLS), D_MODEL ** -0.5),
        'rwkv_mu': jax.random.uniform(next(ks), (L, RW_PROJ), F32, 0.0, 1.0),
        'w0': jax.random.uniform(next(ks), (L, RW_WIDTH), F32, -4.0, 1.0),
        'w2': nrm((L, DECAY_LORA, RW_WIDTH), 0.1 * DECAY_LORA ** -0.5),
        'a0': nrm((L, RW_WIDTH), 0.1),
        'a2': nrm((L, ICLR_LORA, RW_WIDTH), 0.1 * ICLR_LORA ** -0.5),
        'g2': nrm((L, GATE_LORA, RW_WIDTH), GATE_LORA ** -0.5),
        'k_k': 0.85 + nrm((L, RW_WIDTH), 0.01),
        'k_a': 1.0 + nrm((L, RW_WIDTH), 0.01),
        'r_k': nrm((L, RW_HEADS, RW_HEAD_DIM), 0.1),
        'lnx_g': 1.0 + nrm((L, RW_WIDTH), 0.01),
        'lnx_b': nrm((L, RW_WIDTH), 0.01),
        'up_rw': nrm((L, RW_WIDTH, D_MODEL), RW_WIDTH ** -0.5),
        'up_mb': nrm((L, MB_WIDTH, D_MODEL), MB_WIDTH ** -0.5),
        'w_o': nrm((L, D_MODEL, D_MODEL), D_MODEL ** -0.5),
        'norm_ffn': 1.0 + nrm((L, D_MODEL), 0.01),
        'w_gate': nrm((L, D_MODEL, D_FF), D_MODEL ** -0.5),
        'w_up': nrm((L, D_MODEL, D_FF), D_MODEL ** -0.5),
        'w_down': nrm((L, D_FF, D_MODEL), D_FF ** -0.5),
        'norm_final': 1.0 + nrm((D_MODEL,), 0.01),
    }


def reference(x_prompt, x_sample, cache_k, cache_v, state_wkv, state_shift, page_table,
              norm_mix, w_in, rwkv_mu, w0, w2, a0, a2, g2, k_k, k_a, r_k, lnx_g, lnx_b,
              up_rw, up_mb, w_o, norm_ffn, w_gate, w_up, w_down, norm_final):
    hp, hs = x_prompt, x_sample
    nb_prompt = x_prompt.shape[0]
    zero_wkv = jnp.zeros((nb_prompt, RW_HEADS, RW_HEAD_DIM, RW_HEAD_DIM), F32)
    zero_shift = jnp.zeros((nb_prompt, RW_PROJ), x_prompt.dtype)
    kp_l, vp_l, sp_l, shp_l = [], [], [], []
    ks_l, vs_l, ss_l, shs_l = [], [], [], []
    for i in range(DEPTH):
        lw = (norm_mix[i], w_in[i], rwkv_mu[i], w0[i], w2[i], a0[i], a2[i], g2[i], k_k[i], k_a[i],
              r_k[i], lnx_g[i], lnx_b[i], up_rw[i], up_mb[i], w_o[i], norm_ffn[i], w_gate[i], w_up[i], w_down[i])
        hp, (kp, vp, sp, shp) = _layer(hp, zero_shift, zero_wkv, _moba_prompt, *lw)
        moba_s = functools.partial(_moba_sample, cache_k=cache_k[i], cache_v=cache_v[i], page_table=page_table)
        hs, (ks_, vs_, ss, shs) = _layer(hs, state_shift[i], state_wkv[i], moba_s, *lw)
        kp_l.append(kp); vp_l.append(vp); sp_l.append(sp); shp_l.append(shp)
        ks_l.append(ks_); vs_l.append(vs_); ss_l.append(ss); shs_l.append(shs)
    y_prompt = _rmsnorm(hp, norm_final)
    y_sample = _rmsnorm(hs, norm_final)
    k_prompt, v_prompt = jnp.stack(kp_l), jnp.stack(vp_l)
    wkv_prompt, shift_prompt = jnp.stack(sp_l), jnp.stack(shp_l)
    k_sample, v_sample = jnp.stack(ks_l), jnp.stack(vs_l)
    wkv_sample, shift_sample = jnp.stack(ss_l), jnp.stack(shs_l)
    return (y_prompt, y_sample, k_prompt, v_prompt, wkv_prompt, shift_prompt,
            k_sample, v_sample, wkv_sample, shift_sample)
```

```python
import functools

import jax
import jax.numpy as jnp
from jax import lax
from jax.experimental import pallas as pl
from jax.experimental.pallas import tpu as pltpu

F32 = jnp.float32
BF16 = jnp.bfloat16

RMS_EPS = 1e-6
LNX_EPS = 64e-5
MB_BLOCK = 256
MB_TOPK = 3
MB_HEAD_DIM = 128
PAGE_SIZE = 128
RW_HEAD_DIM = 64
DECAY_LORA = 96
ICLR_LORA = 96
GATE_LORA = 256
LORA_TAIL = 512
SCAN_CHUNK = 64
LANES = 128
VMEM_LIMIT = 56 * 1024 * 1024

_NEG_INF = float("-inf")


def _params(n_axes):
    return pltpu.CompilerParams(dimension_semantics=("arbitrary",) * n_axes,
                                vmem_limit_bytes=VMEM_LIMIT)


def _dot(a, b):
    return jnp.dot(a, b, preferred_element_type=F32)


def _dot_nt(a, b):
    return lax.dot_general(a, b, (((1,), (1,)), ((), ())), preferred_element_type=F32)


def _dot_tn(a, b):
    return lax.dot_general(a, b, (((0,), (0,)), ((), ())), preferred_element_type=F32)


def _split2(x):
    hi = x.astype(BF16)
    lo = (x - hi.astype(F32)).astype(BF16)
    return hi, lo


def _split3(x):
    hi = x.astype(BF16)
    r1 = x - hi.astype(F32)
    mid = r1.astype(BF16)
    lo = (r1 - mid.astype(F32)).astype(BF16)
    return hi, mid, lo


def _sigmoid(x):
    return 1.0 / (1.0 + jnp.exp(-x))


def _softplus(x):
    return jnp.maximum(x, 0.0) + jnp.log1p(jnp.exp(-jnp.abs(x)))


def _rmsnorm_body(x_ref, g_ref, o_ref):
    x = x_ref[...]
    ms = jnp.mean(x * x, axis=-1, keepdims=True)
    o_ref[...] = (x * lax.rsqrt(ms + RMS_EPS) * g_ref[...]).astype(o_ref.dtype)


def _rmsnorm(x, g, out_dtype, tm=256):
    m, d = x.shape
    tm = min(tm, m)
    assert m % tm == 0
    return pl.pallas_call(
        _rmsnorm_body,
        grid=(m // tm,),
        in_specs=[pl.BlockSpec((tm, d), lambda i: (i, 0)), pl.BlockSpec((1, d), lambda i: (0, 0))],
        out_specs=pl.BlockSpec((tm, d), lambda i: (i, 0)),
        out_shape=jax.ShapeDtypeStruct((m, d), out_dtype),
        compiler_params=_params(1),
        name="rmsnorm",
    )(x, g.reshape(1, d))


def _pick_tile(n, pref):
    t = pref
    while n % t:
        t //= 2
    assert t >= LANES
    return t


def _fused_matmul(name, xp, xs, ws, w_col0, n_out, tn, tm, extras, epilogue, out_dtypes, x_of=None):
    nd, ne, no, nx = len(ws), len(extras), len(out_dtypes), len(xp)
    x_of = list(range(nd)) if x_of is None else x_of
    mp, ms = xp[0].shape[0], xs[0].shape[0]
    tm = min(tm, mp)
    assert mp % tm == 0 and n_out % tn == 0
    nj, ni = n_out // tn, mp // tm
    for c in w_col0:
        assert c % tn == 0
    cast = [w.dtype != BF16 for w in ws]
    n_cast = sum(cast)
    cast_slot = [sum(cast[:k]) for k in range(nd)]

    def body(*refs):
        xp_r = refs[:nx]
        xs_r = refs[nx:2 * nx]
        w_r = refs[2 * nx:2 * nx + nd]
        base = 2 * nx + nd
        ep_r = refs[base:base + ne]
        es_r = refs[base + ne:base + 2 * ne]
        base += 2 * ne
        op_r = refs[base:base + no]
        os_r = refs[base + no:base + 2 * no]
        wb_r = refs[base + 2 * no:]
        i = pl.program_id(1)

        def weight(k):
            return wb_r[cast_slot[k]][...] if cast[k] else w_r[k][...]

        @pl.when(i == 0)
        def _():
            for k in range(nd):
                if cast[k]:
                    wb_r[cast_slot[k]][...] = w_r[k][...].astype(BF16)
            accs = [_dot(xs_r[x_of[k]][...], weight(k)) for k in range(nd)]
            outs = epilogue(accs, [e[...] for e in es_r])
            for o, val in zip(os_r, outs):
                o[...] = val.astype(o.dtype)

        accs = [_dot(xp_r[x_of[k]][...], weight(k)) for k in range(nd)]
        outs = epilogue(accs, [e[...] for e in ep_r])
        for o, val in zip(op_r, outs):
            o[...] = val.astype(o.dtype)

    in_specs = []
    for k in range(nx):
        kk = xp[k].shape[1]
        in_specs.append(pl.BlockSpec((tm, kk), lambda j, i: (i, 0)))
    for k in range(nx):
        kk = xs[k].shape[1]
        in_specs.append(pl.BlockSpec((ms, kk), lambda j, i: (0, 0)))
    for k in range(nd):
        kk = ws[k].shape[0]
        in_specs.append(pl.BlockSpec((kk, tn), functools.partial(lambda j, i, c: (0, j + c), c=w_col0[k] // tn)))
    for (_, _, c0) in extras:
        assert c0 % tn == 0
        in_specs.append(pl.BlockSpec((tm, tn), functools.partial(lambda j, i, c: (i, j + c), c=c0 // tn)))
    for (_, _, c0) in extras:
        in_specs.append(pl.BlockSpec((ms, tn), functools.partial(lambda j, i, c: (0, j + c), c=c0 // tn)))
    out_specs = [pl.BlockSpec((tm, tn), lambda j, i: (i, j)) for _ in range(no)]
    out_specs += [pl.BlockSpec((ms, tn), lambda j, i: (0, j)) for _ in range(no)]
    out_shape = [jax.ShapeDtypeStruct((mp, n_out), dt) for dt in out_dtypes]
    out_shape += [jax.ShapeDtypeStruct((ms, n_out), dt) for dt in out_dtypes]
    scratch = [pltpu.VMEM((ws[k].shape[0], tn), BF16) for k in range(nd) if cast[k]]
    assert len(scratch) == n_cast
    args = list(xp) + list(xs) + list(ws) + [e[0] for e in extras] + [e[1] for e in extras]
    outs = pl.pallas_call(
        body,
        grid=(nj, ni),
        in_specs=in_specs,
        out_specs=out_specs,
        out_shape=out_shape,
        scratch_shapes=scratch,
        compiler_params=_params(2),
        name=name,
    )(*args)
    return outs[:no], outs[no:]


def _ep_identity(accs, extras):
    return (accs[0],)


def _ep_sigmoid(accs, extras):
    return (_sigmoid(accs[0]),)


def _ep_merge(accs, extras):
    return (extras[0].astype(F32) * accs[0] + extras[1].astype(F32) * accs[1],)


def _ep_residual(accs, extras):
    return (extras[0] + accs[0],)


def _ep_swiglu(accs, extras):
    g = accs[0]
    return (g * _sigmoid(g) * accs[1],)


def _rw_prep(p, prev, mu, w0, a0, w2p, a2p, g2p, width):
    pm = p + (prev - p) * mu
    r = pm[:, :width]
    k = pm[:, width:2 * width]
    v = pm[:, 2 * width:3 * width]
    tail = pm[:, 3 * width:3 * width + LORA_TAIL]
    lw = w0 + _dot(jnp.tanh(tail).astype(BF16), w2p)
    w_log = -_softplus(-lw) - 0.5
    logd = -jnp.exp(w_log)
    a_lr = _sigmoid(a0 + _dot(tail.astype(BF16), a2p))
    g = _dot(_sigmoid(tail).astype(BF16), g2p)
    return r, k, v, logd, a_lr, g


def _seg_ones():
    ri = lax.broadcasted_iota(jnp.int32, (LANES, LANES), 0) // RW_HEAD_DIM
    ci = lax.broadcasted_iota(jnp.int32, (LANES, LANES), 1) // RW_HEAD_DIM
    return jnp.where(ri == ci, 1.0, 0.0).astype(BF16)


def _seg_sum(x, ones_bd):
    hi, lo = _split2(x)
    return _dot(hi, ones_bd) + _dot(lo, ones_bd)


def _rw_scan_body(p_ref, mu_ref, w0_ref, a0_ref, kk_ref, ka_ref, rk_ref, lg_ref, lb_ref,
                  w2_ref, a2_ref, g2_ref, y_ref, st_ref, pbuf_ref, state_ref, *, width):
    c = pl.program_id(0)
    ch = SCAN_CHUNK
    n_pair = width // LANES

    @pl.when(c == 0)
    def _():
        pbuf_ref[...] = jnp.zeros_like(pbuf_ref)
        state_ref[...] = jnp.zeros_like(state_ref)

    pbuf_ref[pl.ds(8, ch), :] = p_ref[...]
    p = pbuf_ref[pl.ds(8, ch), :]
    prev = pbuf_ref[pl.ds(7, ch), :]
    r, k, v, logd, a_lr, g = _rw_prep(p, prev, mu_ref[...], w0_ref[...], a0_ref[...],
                                      w2_ref[...], a2_ref[...], g2_ref[...], width)
    pbuf_ref[pl.ds(0, 8), :] = pbuf_ref[pl.ds(ch, 8), :]

    ti = lax.broadcasted_iota(jnp.int32, (ch, ch), 0)
    tj = lax.broadcasted_iota(jnp.int32, (ch, ch), 1)
    ltri = jnp.where(ti >= tj, 1.0, 0.0).astype(BF16)
    d_hi, d_mid, d_lo = _split3(logd)
    lc = _dot(ltri, d_hi) + _dot(ltri, d_mid) + _dot(ltri, d_lo)
    lc_end = lc[ch - 1:ch, :]
    w_t = jnp.exp(lc)
    w_tm1 = jnp.exp(lc - logd)
    w_inv = jnp.exp(-lc)
    w_rem = jnp.exp(lc_end - lc)
    w_end = jnp.exp(lc_end)

    ones_bd = _seg_ones()
    kkr = k * kk_ref[...]
    kf = k * (1.0 + (a_lr - 1.0) * ka_ref[...])
    rkk = r * kf * rk_ref[...]

    lane = lax.broadcasted_iota(jnp.int32, (ch, LANES), 1)
    m0 = jnp.where(lane < RW_HEAD_DIM, 1.0, 0.0)
    m1 = 1.0 - m0

    def stack(z):
        return jnp.concatenate([z * m0, z * m1], axis=0)

    ri = lax.broadcasted_iota(jnp.int32, (2 * ch, 2 * ch), 0) % ch
    ci = lax.broadcasted_iota(jnp.int32, (2 * ch, 2 * ch), 1) % ch
    tri_strict = jnp.where(ri > ci, 1.0, 0.0)
    tri_incl = jnp.where(ri >= ci, 1.0, 0.0)
    er = lax.broadcasted_iota(jnp.int32, (LANES, LANES), 0)
    ec = lax.broadcasted_iota(jnp.int32, (LANES, LANES), 1)
    eye = jnp.where(er == ec, 1.0, 0.0)

    for gp in range(n_pair):
        sl = slice(gp * LANES, (gp + 1) * LANES)
        ss = _seg_sum(kkr[:, sl] * kkr[:, sl], ones_bd)
        kk = kkr[:, sl] / jnp.maximum(jnp.sqrt(ss), 1e-12)
        a_t = -kk * w_tm1[:, sl]
        b_raw = kk * a_lr[:, sl]
        r_t = r[:, sl] * w_t[:, sl]
        b_t = b_raw * w_inv[:, sl]
        k_t = kf[:, sl] * w_inv[:, sl]
        b_h = b_raw * w_rem[:, sl]
        k_h = kf[:, sl] * w_rem[:, sl]
        vv = v[:, sl]

        a_s = stack(a_t).astype(BF16)
        r_s = stack(r_t).astype(BF16)
        v_s = stack(vv).astype(BF16)
        lhs = jnp.concatenate([a_s, r_s], axis=0)
        rhs = jnp.concatenate([stack(b_t), stack(k_t)], axis=0).astype(BF16)
        gram = _dot_nt(lhs, rhs)
        h2 = 2 * ch
        m_bd = gram[:h2, :h2] * tri_strict
        n_bd = gram[:h2, h2:] * tri_strict
        pb = gram[h2:, :h2] * tri_incl
        pk = gram[h2:, h2:] * tri_incl

        st = state_ref[gp]
        st_b = st.astype(BF16)
        x = _dot(jnp.concatenate([a_s, n_bd.astype(BF16)], axis=1),
                 jnp.concatenate([st_b, v_s], axis=0))
        sa = x
        pw = m_bd.astype(BF16)
        n_steps = ch.bit_length() - 1
        for s in range(n_steps):
            sa = sa + _dot(pw, sa.astype(BF16))
            if s + 1 < n_steps:
                pw = _dot(pw, pw).astype(BF16)
        sa_b = sa.astype(BF16)
        y_s = _dot(jnp.concatenate([r_s, pb.astype(BF16), pk.astype(BF16)], axis=1),
                   jnp.concatenate([st_b, sa_b, v_s], axis=0))
        y = y_s[:ch] + y_s[ch:]

        w_col = jnp.sum(eye * w_end[:, sl], axis=1, keepdims=True)
        upd = _dot_tn(jnp.concatenate([stack(b_h), stack(k_h)], axis=0).astype(BF16),
                      jnp.concatenate([sa_b, v_s], axis=0))
        state_ref[gp] = st * w_col + upd

        mean = _seg_sum(y, ones_bd) * (1.0 / RW_HEAD_DIM)
        d = y - mean
        var = _seg_sum(d * d, ones_bd) * (1.0 / RW_HEAD_DIM)
        yn = d * lax.rsqrt(var + LNX_EPS) * lg_ref[:, sl] + lb_ref[:, sl]
        bonus = _seg_sum(rkk[:, sl], ones_bd) * vv
        y_ref[:, sl] = ((yn + bonus) * g[:, sl]).astype(y_ref.dtype)

    @pl.when(c == pl.num_programs(0) - 1)
    def _():
        st_ref[...] = state_ref[...]


def _rw_scan_prompt(p_rw, prm, width):
    t, pw = p_rw.shape
    ch = SCAN_CHUNK
    assert t % ch == 0 and width % LANES == 0
    n_pair = width // LANES
    row = lambda n: pl.BlockSpec((1, n), lambda c: (0, 0))
    full = lambda a: pl.BlockSpec(a.shape, lambda c: (0, 0))
    return pl.pallas_call(
        functools.partial(_rw_scan_body, width=width),
        grid=(t // ch,),
        in_specs=[pl.BlockSpec((ch, pw), lambda c: (c, 0)), row(pw)] + [row(width)] * 7
                 + [full(prm["w2p"]), full(prm["a2p"]), full(prm["g2p"])],
        out_specs=[pl.BlockSpec((ch, width), lambda c: (c, 0)),
                   pl.BlockSpec((n_pair, LANES, LANES), lambda c: (0, 0, 0))],
        out_shape=[jax.ShapeDtypeStruct((t, width), BF16),
                   jax.ShapeDtypeStruct((n_pair, LANES, LANES), F32)],
        scratch_shapes=[pltpu.VMEM((ch + 8, pw), F32), pltpu.VMEM((n_pair, LANES, LANES), F32)],
        compiler_params=_params(1),
        name="rwkv_scan",
    )(p_rw, prm["mu"], prm["w0"], prm["a0"], prm["k_k"], prm["k_a"], prm["r_k"], prm["lnx_g"], prm["lnx_b"],
      prm["w2p"], prm["a2p"], prm["g2p"])


def _rw_sample_prep_body(p_ref, prev_ref, mu_ref, w0_ref, a0_ref, w2_ref, a2_ref, g2_ref,
                         r_ref, k_ref, v_ref, d_ref, alr_ref, g_ref, *, width):
    r, k, v, logd, a_lr, g = _rw_prep(p_ref[...], prev_ref[...], mu_ref[...], w0_ref[...], a0_ref[...],
                                      w2_ref[...], a2_ref[...], g2_ref[...], width)
    r_ref[...] = r
    k_ref[...] = k
    v_ref[...] = v
    d_ref[...] = logd
    alr_ref[...] = a_lr
    g_ref[...] = g


def _rw_sample_prep(p_s, prev_s, prm, width):
    b = p_s.shape[0]
    args = (p_s, prev_s, prm["mu"], prm["w0"], prm["a0"], prm["w2p"], prm["a2p"], prm["g2p"])
    return pl.pallas_call(
        functools.partial(_rw_sample_prep_body, width=width),
        grid=(1,),
        in_specs=[pl.BlockSpec(a.shape, lambda i: (0, 0)) for a in args],
        out_specs=[pl.BlockSpec((b, width), lambda i: (0, 0))] * 6,
        out_shape=[jax.ShapeDtypeStruct((b, width), F32)] * 6,
        compiler_params=_params(1),
        name="rwkv_sample_prep",
    )(*args)


def _rw_sample_step_body(s_ref, r_ref, k_ref, v_ref, d_ref, alr_ref, g_ref,
                         kk_ref, ka_ref, rk_ref, lg_ref, lb_ref, so_ref, y_ref):
    s = s_ref[0]
    r, k, v = r_ref[0], k_ref[0], v_ref[0]
    a_lr, g = alr_ref[0], g_ref[0]
    w = jnp.exp(d_ref[0])
    kkr = k * kk_ref[...]
    kk = kkr / jnp.maximum(jnp.sqrt(jnp.sum(kkr * kkr, axis=-1, keepdims=True)), 1e-12)
    kf = k * (1.0 + (a_lr - 1.0) * ka_ref[...])
    a = -kk
    b = kk * a_lr
    n = RW_HEAD_DIM
    eye = (lax.broadcasted_iota(jnp.int32, (n, n), 0) == lax.broadcasted_iota(jnp.int32, (n, n), 1))
    sa = jnp.sum(s * a, axis=-1, keepdims=True)
    v_col = jnp.sum(jnp.where(eye, v, 0.0), axis=-1, keepdims=True)
    s1 = s * w + sa * b + v_col * kf
    so_ref[0] = s1
    y_col = jnp.sum(s1 * r, axis=-1, keepdims=True)
    y = jnp.sum(jnp.where(eye, y_col, 0.0), axis=-2, keepdims=True)
    mean = jnp.mean(y, axis=-1, keepdims=True)
    d = y - mean
    var = jnp.mean(d * d, axis=-1, keepdims=True)
    yn = d * lax.rsqrt(var + LNX_EPS) * lg_ref[...] + lb_ref[...]
    bonus = jnp.sum(r * kf * rk_ref[...], axis=-1, keepdims=True) * v
    y_ref[0] = (yn + bonus) * g


def _rw_sample_step(state, rows, head_prm):
    b, h, n, _ = state.shape
    vec = pl.BlockSpec((1, h, 1, n), lambda i: (i, 0, 0, 0))
    par = pl.BlockSpec((h, 1, n), lambda i: (0, 0, 0))
    mat = pl.BlockSpec((1, h, n, n), lambda i: (i, 0, 0, 0))
    return pl.pallas_call(
        _rw_sample_step_body,
        grid=(b,),
        in_specs=[mat] + [vec] * 6 + [par] * 5,
        out_specs=[mat, vec],
        out_shape=[jax.ShapeDtypeStruct((b, h, n, n), F32), jax.ShapeDtypeStruct((b, h, 1, n), F32)],
        compiler_params=_params(1),
        name="rwkv_sample_step",
    )(state, *rows, *head_prm)


def _top_rank(gs, n_valid):
    nb = gs.shape[0]
    rowi = lax.broadcasted_iota(jnp.int32, gs.shape, 0)
    valid = rowi < n_valid
    gm = jnp.where(valid, gs, _NEG_INF)
    rank = jnp.zeros(gs.shape, F32)
    for m in range(nb):
        row = gm[m:m + 1, :]
        beats = jnp.where(row > gm, 1.0, jnp.where(row == gm, jnp.where(rowi > m, 1.0, 0.0), 0.0))
        rank = rank + beats
    return jnp.where(valid, jnp.where(rank < MB_TOPK, 1.0, 0.0), 0.0)


def _moba_prompt_body(q_ref, k_ref, v_ref, o_ref, kb_ref, vt_ref, km_ref, sel_ref, *, nb):
    i = pl.program_id(1)
    blk, dh = MB_BLOCK, MB_HEAD_DIM

    @pl.when(i == 0)
    def _():
        for n in range(nb):
            kn = k_ref[n * blk:(n + 1) * blk, :]
            kb_ref[n] = kn.astype(BF16)
            km_ref[n:n + 1, :] = jnp.mean(kn, axis=0, keepdims=True)
            vt_ref[n] = v_ref[n * blk:(n + 1) * blk, :].T.astype(BF16)

    q_t = q_ref[...].T
    q_hi, q_lo = _split2(q_t)
    k_hi, k_lo = _split2(km_ref[...])
    gs = _dot(k_hi, q_hi) + _dot(k_hi, q_lo) + _dot(k_lo, q_hi)
    sel_ref[...] = _top_rank(gs, i)

    q_s = (q_t * (dh ** -0.5)).astype(BF16)
    s = _dot(kb_ref[i], q_s)
    kr = lax.broadcasted_iota(jnp.int32, (blk, blk), 0)
    qc = lax.broadcasted_iota(jnp.int32, (blk, blk), 1)
    s = jnp.where(kr <= qc, s, _NEG_INF)
    m = jnp.max(s, axis=0, keepdims=True)
    p = jnp.exp(s - m)
    l = jnp.sum(p, axis=0, keepdims=True)
    acc = _dot(vt_ref[i], p.astype(BF16))

    def step(j, carry):
        m, l, acc = carry
        s = _dot(kb_ref[j], q_s)
        s = jnp.where(sel_ref[pl.ds(j, 1), :] > 0.0, s, _NEG_INF)
        m_new = jnp.maximum(m, jnp.max(s, axis=0, keepdims=True))
        p = jnp.exp(s - m_new)
        alpha = jnp.exp(m - m_new)
        l = alpha * l + jnp.sum(p, axis=0, keepdims=True)
        acc = alpha * acc + _dot(vt_ref[j], p.astype(BF16))
        return m_new, l, acc

    m, l, acc = lax.fori_loop(0, i, step, (m, l, acc))
    o_ref[...] = (acc * (1.0 / l)).T.astype(o_ref.dtype)


def _moba_prompt(q, k, v, n_heads):
    t = q.shape[0]
    blk, dh = MB_BLOCK, MB_HEAD_DIM
    assert t % blk == 0
    nb = t // blk
    return pl.pallas_call(
        functools.partial(_moba_prompt_body, nb=nb),
        grid=(n_heads, nb),
        in_specs=[pl.BlockSpec((blk, dh), lambda h, i: (i, h)),
                  pl.BlockSpec((t, dh), lambda h, i: (0, h)),
                  pl.BlockSpec((t, dh), lambda h, i: (0, h))],
        out_specs=pl.BlockSpec((blk, dh), lambda h, i: (i, h)),
        out_shape=jax.ShapeDtypeStruct((t, n_heads * dh), BF16),
        scratch_shapes=[pltpu.VMEM((nb, blk, dh), BF16), pltpu.VMEM((nb, dh, blk), BF16),
                        pltpu.VMEM((nb, dh), F32), pltpu.VMEM((nb, blk), F32)],
        compiler_params=_params(2),
        name="moba_prompt",
    )(q, k, v)


def _page_mean_body(pt_ref, p0_ref, p1_ref, o_ref):
    n = pl.program_id(1)
    tot = jnp.sum(p0_ref[0], axis=0, keepdims=True) + jnp.sum(p1_ref[0], axis=0, keepdims=True)
    o_ref[0, pl.ds(n, 1), :] = tot * (1.0 / MB_BLOCK)


def _page_means(cache_k3, page_table):
    b, n_pages = page_table.shape
    ppb = MB_BLOCK // PAGE_SIZE
    assert ppb == 2 and n_pages % ppb == 0
    nblk = n_pages // ppb
    width = cache_k3.shape[2]
    grid_spec = pltpu.PrefetchScalarGridSpec(
        num_scalar_prefetch=1,
        grid=(b, nblk),
        in_specs=[pl.BlockSpec((1, PAGE_SIZE, width), lambda s, n, pt: (pt[s, 2 * n], 0, 0)),
                  pl.BlockSpec((1, PAGE_SIZE, width), lambda s, n, pt: (pt[s, 2 * n + 1], 0, 0))],
        out_specs=pl.BlockSpec((1, nblk, width), lambda s, n, pt: (s, 0, 0)),
    )
    return pl.pallas_call(
        _page_mean_body,
        grid_spec=grid_spec,
        out_shape=jax.ShapeDtypeStruct((b, nblk, width), F32),
        compiler_params=_params(2),
        name="moba_page_means",
    )(page_table, cache_k3, cache_k3)


def _sample_gate_body(q_ref, km_ref, o_ref, *, n_heads):
    prod = km_ref[0] * q_ref[0]
    width = prod.shape[1]
    ri = lax.broadcasted_iota(jnp.int32, (width, LANES), 0) // MB_HEAD_DIM
    ci = lax.broadcasted_iota(jnp.int32, (width, LANES), 1)
    head_ind = jnp.where(ri == ci, 1.0, 0.0).astype(BF16)
    hi, mid, lo = _split3(prod)
    gs = _dot(hi, head_ind) + _dot(mid, head_ind) + _dot(lo, head_ind)
    nblk = gs.shape[0]
    sel = _top_rank(gs, nblk)
    rowi = lax.broadcasted_iota(jnp.int32, gs.shape, 0)
    rank = jnp.zeros(gs.shape, F32)
    for m in range(nblk):
        row = gs[m:m + 1, :]
        rank = rank + jnp.where(row > gs, 1.0, jnp.where(row == gs, jnp.where(rowi > m, 1.0, 0.0), 0.0))
    rows = []
    for t in range(MB_TOPK):
        pick = jnp.where(rank == float(t), sel, 0.0) * rowi.astype(F32)
        rows.append(jnp.sum(pick, axis=0, keepdims=True))
    rows.append(jnp.zeros((8 - MB_TOPK, LANES), F32))
    o_ref[0] = jnp.concatenate(rows, axis=0).astype(jnp.int32)


def _sample_gate(q_s3, kmean, n_heads):
    b, nblk, width = kmean.shape
    return pl.pallas_call(
        functools.partial(_sample_gate_body, n_heads=n_heads),
        grid=(b,),
        in_specs=[pl.BlockSpec((1, 1, width), lambda s: (s, 0, 0)),
                  pl.BlockSpec((1, nblk, width), lambda s: (s, 0, 0))],
        out_specs=pl.BlockSpec((1, 8, LANES), lambda s: (s, 0, 0)),
        out_shape=jax.ShapeDtypeStruct((b, 8, LANES), jnp.int32),
        compiler_params=_params(1),
        name="moba_sample_gate",
    )(q_s3, kmean)


def _sample_attn_body(pt_ref, top_ref, q_ref, kn_ref, vn_ref, *refs):
    n_pg = MB_TOPK * 2
    k_refs = refs[:n_pg]
    v_refs = refs[n_pg:2 * n_pg]
    o_ref = refs[2 * n_pg]
    dh = MB_HEAD_DIM
    q = q_ref[0]
    q8 = jnp.broadcast_to(q * (dh ** -0.5), (8, dh)).astype(BF16)
    keys = jnp.concatenate([r[0] for r in k_refs], axis=0).astype(BF16)
    vals = jnp.concatenate([r[0] for r in v_refs], axis=0).astype(BF16)
    s = _dot_nt(q8, keys)[0:1, :]
    s_self = _dot_nt(q8, jnp.broadcast_to(kn_ref[0], (8, dh)).astype(BF16))[0:1, 0:1]
    m = jnp.maximum(jnp.max(s, axis=-1, keepdims=True), s_self)
    p = jnp.exp(s - m)
    p_self = jnp.exp(s_self - m)
    l = jnp.sum(p, axis=-1, keepdims=True) + p_self
    pv = _dot(jnp.broadcast_to(p, (8, p.shape[1])).astype(BF16), vals)[0:1, :]
    o_ref[0] = (pv + vn_ref[0] * p_self) * (1.0 / l)


def _sample_attn(q_s3, k_s3, v_s3, cache_k3, cache_v3, page_table, top, n_heads):
    b = q_s3.shape[0]
    dh = MB_HEAD_DIM

    def page_map(t, half):
        return lambda s, h, pt, tp: (pt[s, 2 * tp[s, h * MB_TOPK + t] + half], 0, h)

    row = pl.BlockSpec((1, 1, dh), lambda s, h, pt, tp: (s, 0, h))
    page_specs = [pl.BlockSpec((1, PAGE_SIZE, dh), page_map(t, half))
                  for t in range(MB_TOPK) for half in range(2)]
    grid_spec = pltpu.PrefetchScalarGridSpec(
        num_scalar_prefetch=2,
        grid=(b, n_heads),
        in_specs=[row, row, row] + page_specs + page_specs,
        out_specs=row,
    )
    n_pg = MB_TOPK * 2
    return pl.pallas_call(
        _sample_attn_body,
        grid_spec=grid_spec,
        out_shape=jax.ShapeDtypeStruct((b, 1, n_heads * dh), F32),
        compiler_params=_params(2),
        name="moba_sample_attn",
    )(page_table, top, q_s3, k_s3, v_s3, *([cache_k3] * n_pg), *([cache_v3] * n_pg))


def _layer(xp, xs, cache_k, cache_v, wkv_s, shift_s, page_table, norm_mix, w_in, rwkv_mu, w0, w2, a0, a2, g2,
           k_k, k_a, r_k, lnx_g, lnx_b, up_rw, up_mb, w_o, norm_ffn, w_gate, w_up, w_down):
    t, d_model = xp.shape
    b = xs.shape[0]
    n_rw_heads = r_k.shape[0]
    rw_width = n_rw_heads * RW_HEAD_DIM
    n_mb_heads = cache_k.shape[2]
    mb_width = n_mb_heads * MB_HEAD_DIM
    rw_proj = rwkv_mu.shape[0]
    lora = DECAY_LORA + ICLR_LORA + GATE_LORA
    assert rw_proj == 3 * rw_width + lora
    rw_pad = 3 * rw_width + LORA_TAIL
    rw_col0 = 3 * mb_width
    gate_col0 = rw_col0 + rw_proj
    d_ff = w_gate.shape[1]

    xn_p = _rmsnorm(xp, norm_mix, BF16)
    xn_s = _rmsnorm(xs, norm_mix, BF16)

    def proj(name, col0, n_out, tn, ep=_ep_identity, w=w_in, dt=F32):
        (o_p,), (o_s,) = _fused_matmul(name, [xn_p], [xn_s], [w], [col0], n_out, _pick_tile(n_out, tn), 1024,
                                       [], ep, [dt])
        return o_p, o_s

    q_p, q_s = proj("proj_q", 0, mb_width, 512)
    k_p, k_s = proj("proj_k", mb_width, mb_width, 512)
    v_p, v_s = proj("proj_v", 2 * mb_width, mb_width, 512)
    p_rw_p, p_rw_s = proj("proj_rw", rw_col0, rw_pad, 512)
    w_gates = w_in[:, gate_col0:].astype(BF16)
    sg_p, sg_s = proj("proj_gates", 0, 2 * d_model, 512, ep=_ep_sigmoid, w=w_gates, dt=BF16)

    pad_row = lambda a: jnp.pad(a, (0, rw_pad - rw_proj)).reshape(1, rw_pad)
    lora_pad = lambda a, r0: jnp.pad(a, ((r0, LORA_TAIL - r0 - a.shape[0]), (0, 0))).astype(BF16)
    prm = dict(
        mu=pad_row(rwkv_mu), w0=w0.reshape(1, rw_width), a0=a0.reshape(1, rw_width),
        k_k=k_k.reshape(1, rw_width), k_a=k_a.reshape(1, rw_width), r_k=r_k.reshape(1, rw_width),
        lnx_g=lnx_g.reshape(1, rw_width), lnx_b=lnx_b.reshape(1, rw_width),
        w2p=lora_pad(w2, 0), a2p=lora_pad(a2, DECAY_LORA), g2p=lora_pad(g2, DECAY_LORA + ICLR_LORA))
    y_rw_p, st_pairs = _rw_scan_prompt(p_rw_p, prm, rw_width)
    n_pair = rw_width // LANES
    st5 = st_pairs.reshape(n_pair, 2, RW_HEAD_DIM, 2, RW_HEAD_DIM)
    wkv_p = jnp.stack([st5[:, 0, :, 0, :], st5[:, 1, :, 1, :]], axis=1)
    wkv_p = jnp.swapaxes(wkv_p, -1, -2).reshape(n_rw_heads, RW_HEAD_DIM, RW_HEAD_DIM)
    shift_p = p_rw_p[t - 1, :rw_proj]

    prev_s = jnp.pad(shift_s, ((0, 0), (0, rw_pad - rw_proj)))
    rows = _rw_sample_prep(p_rw_s, prev_s, prm, rw_width)
    rows = [a.reshape(b, n_rw_heads, 1, RW_HEAD_DIM) for a in rows]
    head_prm = [a.reshape(n_rw_heads, 1, RW_HEAD_DIM) for a in (k_k, k_a, r_k, lnx_g, lnx_b)]
    wkv_s_new, y_rw_s = _rw_sample_step(wkv_s, rows, head_prm)
    y_rw_s = y_rw_s.reshape(b, rw_width).astype(BF16)
    shift_s_new = p_rw_s[:, :rw_proj]

    y_mb_p = _moba_prompt(q_p, k_p, v_p, n_mb_heads)
    n_pool = cache_k.shape[0]
    cache_k3 = cache_k.reshape(n_pool, PAGE_SIZE, mb_width)
    cache_v3 = cache_v.reshape(n_pool, PAGE_SIZE, mb_width)
    kmean_s = _page_means(cache_k3, page_table)
    q_s3 = q_s.reshape(b, 1, mb_width)
    top = _sample_gate(q_s3, kmean_s, n_mb_heads)
    top = jnp.swapaxes(top[:, :MB_TOPK, :n_mb_heads], 1, 2).reshape(b, n_mb_heads * MB_TOPK)
    y_mb_s = _sample_attn(q_s3, k_s.reshape(b, 1, mb_width), v_s.reshape(b, 1, mb_width),
                          cache_k3, cache_v3, page_table, top, n_mb_heads)
    y_mb_s = y_mb_s.reshape(b, mb_width).astype(BF16)

    (mg_p,), (mg_s,) = _fused_matmul("merge", [y_rw_p, y_mb_p], [y_rw_s, y_mb_s], [up_rw, up_mb], [0, 0],
                                     d_model, _pick_tile(d_model, 512), 1024,
                                     [(sg_p, sg_s, 0), (sg_p, sg_s, d_model)], _ep_merge, [BF16])
    (x2_p,), (x2_s,) = _fused_matmul("out_proj", [mg_p], [mg_s], [w_o], [0], d_model, _pick_tile(d_model, 512), 1024,
                                     [(xp, xs, 0)], _ep_residual, [F32])
    xn2_p = _rmsnorm(x2_p, norm_ffn, BF16)
    xn2_s = _rmsnorm(x2_s, norm_ffn, BF16)
    (h_p,), (h_s,) = _fused_matmul("ffn_up", [xn2_p], [xn2_s], [w_gate, w_up], [0, 0],
                                   d_ff, _pick_tile(d_ff, 256), 1024, [], _ep_swiglu, [BF16], x_of=[0, 0])
    (x3_p,), (x3_s,) = _fused_matmul("ffn_down", [h_p], [h_s], [w_down], [0], d_model, _pick_tile(d_model, 256), 256,
                                     [(x2_p, x2_s, 0)], _ep_residual, [F32])
    return (x3_p, x3_s, k_p, v_p, wkv_p, shift_p, k_s, v_s, wkv_s_new, shift_s_new)


def kernel(x_prompt, x_sample, cache_k, cache_v, state_wkv, state_shift, page_table, norm_mix, w_in, rwkv_mu, w0, w2, a0, a2, g2, k_k, k_a, r_k, lnx_g, lnx_b, up_rw, up_mb, w_o, norm_ffn, w_gate, w_up, w_down, norm_final):
    depth = w_in.shape[0]
    assert depth == 1 and x_prompt.shape[0] == 1 and x_sample.shape[1] == 1
    t, d_model = x_prompt.shape[1], x_prompt.shape[2]
    b = x_sample.shape[0]
    n_mb_heads = cache_k.shape[3]
    n_rw_heads = r_k.shape[1]
    out = _layer(x_prompt[0], x_sample[:, 0], cache_k[0], cache_v[0], state_wkv[0], state_shift[0], page_table,
                 norm_mix[0], w_in[0], rwkv_mu[0], w0[0], w2[0], a0[0], a2[0], g2[0], k_k[0], k_a[0], r_k[0],
                 lnx_g[0], lnx_b[0], up_rw[0], up_mb[0], w_o[0], norm_ffn[0], w_gate[0], w_up[0], w_down[0])
    x3_p, x3_s, k_p, v_p, wkv_p, shift_p, k_s, v_s, wkv_s, shift_s = out
    y_p = _rmsnorm(x3_p, norm_final, F32)
    y_s = _rmsnorm(x3_s, norm_final, F32)
    rw_proj = rwkv_mu.shape[1]
    return (y_p.reshape(1, t, d_model), y_s.reshape(b, 1, d_model),
            k_p.reshape(1, 1, t, n_mb_heads, MB_HEAD_DIM), v_p.reshape(1, 1, t, n_mb_heads, MB_HEAD_DIM),
            wkv_p.reshape(1, 1, n_rw_heads, RW_HEAD_DIM, RW_HEAD_DIM), shift_p.reshape(1, 1, rw_proj),
            k_s.reshape(1, b, 1, n_mb_heads, MB_HEAD_DIM), v_s.reshape(1, b, 1, n_mb_heads, MB_HEAD_DIM),
            wkv_s.reshape(1, b, n_rw_heads, RW_HEAD_DIM, RW_HEAD_DIM), shift_s.reshape(1, b, rw_proj))
```

```python
import functools

import jax
import jax.numpy as jnp
from jax import lax
from jax.experimental import pallas as pl
from jax.experimental.pallas import tpu as pltpu

F32 = jnp.float32
BF16 = jnp.bfloat16

RMS_EPS = 1e-6
LNX_EPS = 64e-5
MB_BLOCK = 256
MB_TOPK = 3
MB_HEAD_DIM = 128
PAGE_SIZE = 128
RW_HEAD_DIM = 64
DECAY_LORA = 96
ICLR_LORA = 96
GATE_LORA = 256
LORA_TAIL = 512
SCAN_CHUNK = 64
LANES = 128
VMEM_LIMIT = 56 * 1024 * 1024

_NEG_INF = float("-inf")
_LOG2E = 1.4426950408889634


def _params(n_axes):
    return pltpu.CompilerParams(dimension_semantics=("arbitrary",) * n_axes,
                                vmem_limit_bytes=VMEM_LIMIT)


def _dot(a, b):
    return jnp.dot(a, b, preferred_element_type=F32)


def _dot_nt(a, b):
    return lax.dot_general(a, b, (((1,), (1,)), ((), ())), preferred_element_type=F32)


def _dot_tn(a, b):
    return lax.dot_general(a, b, (((0,), (0,)), ((), ())), preferred_element_type=F32)


def _split2(x):
    hi = x.astype(BF16)
    lo = (x - hi.astype(F32)).astype(BF16)
    return hi, lo


def _split3(x):
    hi = x.astype(BF16)
    r1 = x - hi.astype(F32)
    mid = r1.astype(BF16)
    lo = (r1 - mid.astype(F32)).astype(BF16)
    return hi, mid, lo


def _sigmoid(x):
    return 1.0 / (1.0 + jnp.exp(-x))


def _softplus(x):
    return jnp.maximum(x, 0.0) + jnp.log1p(jnp.exp(-jnp.abs(x)))


def _rmsnorm_body(x_ref, g_ref, o_ref):
    x = x_ref[...]
    ms = jnp.mean(x * x, axis=-1, keepdims=True)
    o_ref[...] = (x * lax.rsqrt(ms + RMS_EPS) * g_ref[...]).astype(o_ref.dtype)


def _rmsnorm(x, g, out_dtype, tm=256):
    m, d = x.shape
    tm = min(tm, m)
    assert m % tm == 0
    return pl.pallas_call(
        _rmsnorm_body,
        grid=(m // tm,),
        in_specs=[pl.BlockSpec((tm, d), lambda i: (i, 0)), pl.BlockSpec((1, d), lambda i: (0, 0))],
        out_specs=pl.BlockSpec((tm, d), lambda i: (i, 0)),
        out_shape=jax.ShapeDtypeStruct((m, d), out_dtype),
        compiler_params=_params(1),
        name="rmsnorm",
    )(x, g.reshape(1, d))


def _pick_tile(n, pref):
    t = pref
    while n % t:
        t //= 2
    assert t >= LANES
    return t


def _fused_matmul(name, xp, xs, ws, w_col0, n_out, tn, tm, extras, epilogue, out_dtypes, x_of=None):
    nd, ne, no, nx = len(ws), len(extras), len(out_dtypes), len(xp)
    x_of = list(range(nd)) if x_of is None else x_of
    mp, ms = xp[0].shape[0], xs[0].shape[0]
    tm = min(tm, mp)
    assert mp % tm == 0 and n_out % tn == 0
    nj, ni = n_out // tn, mp // tm
    for c in w_col0:
        assert c % tn == 0
    cast = [w.dtype != BF16 for w in ws]
    n_cast = sum(cast)
    cast_slot = [sum(cast[:k]) for k in range(nd)]

    def body(*refs):
        xp_r = refs[:nx]
        xs_r = refs[nx:2 * nx]
        w_r = refs[2 * nx:2 * nx + nd]
        base = 2 * nx + nd
        ep_r = refs[base:base + ne]
        es_r = refs[base + ne:base + 2 * ne]
        base += 2 * ne
        op_r = refs[base:base + no]
        os_r = refs[base + no:base + 2 * no]
        wb_r = refs[base + 2 * no:]
        i = pl.program_id(1)

        def weight(k):
            return wb_r[cast_slot[k]][...] if cast[k] else w_r[k][...]

        @pl.when(i == 0)
        def _():
            for k in range(nd):
                if cast[k]:
                    wb_r[cast_slot[k]][...] = w_r[k][...].astype(BF16)
            accs = [_dot(xs_r[x_of[k]][...], weight(k)) for k in range(nd)]
            outs = epilogue(accs, [e[...] for e in es_r])
            for o, val in zip(os_r, outs):
                o[...] = val.astype(o.dtype)

        accs = [_dot(xp_r[x_of[k]][...], weight(k)) for k in range(nd)]
        outs = epilogue(accs, [e[...] for e in ep_r])
        for o, val in zip(op_r, outs):
            o[...] = val.astype(o.dtype)

    in_specs = []
    for k in range(nx):
        kk = xp[k].shape[1]
        in_specs.append(pl.BlockSpec((tm, kk), lambda j, i: (i, 0)))
    for k in range(nx):
        kk = xs[k].shape[1]
        in_specs.append(pl.BlockSpec((ms, kk), lambda j, i: (0, 0)))
    for k in range(nd):
        kk = ws[k].shape[0]
        in_specs.append(pl.BlockSpec((kk, tn), functools.partial(lambda j, i, c: (0, j + c), c=w_col0[k] // tn)))
    for (_, _, c0) in extras:
        assert c0 % tn == 0
        in_specs.append(pl.BlockSpec((tm, tn), functools.partial(lambda j, i, c: (i, j + c), c=c0 // tn)))
    for (_, _, c0) in extras:
        in_specs.append(pl.BlockSpec((ms, tn), functools.partial(lambda j, i, c: (0, j + c), c=c0 // tn)))
    out_specs = [pl.BlockSpec((tm, tn), lambda j, i: (i, j)) for _ in range(no)]
    out_specs += [pl.BlockSpec((ms, tn), lambda j, i: (0, j)) for _ in range(no)]
    out_shape = [jax.ShapeDtypeStruct((mp, n_out), dt) for dt in out_dtypes]
    out_shape += [jax.ShapeDtypeStruct((ms, n_out), dt) for dt in out_dtypes]
    scratch = [pltpu.VMEM((ws[k].shape[0], tn), BF16) for k in range(nd) if cast[k]]
    assert len(scratch) == n_cast
    args = list(xp) + list(xs) + list(ws) + [e[0] for e in extras] + [e[1] for e in extras]
    outs = pl.pallas_call(
        body,
        grid=(nj, ni),
        in_specs=in_specs,
        out_specs=out_specs,
        out_shape=out_shape,
        scratch_shapes=scratch,
        compiler_params=_params(2),
        name=name,
    )(*args)
    return outs[:no], outs[no:]


def _ep_identity(accs, extras):
    return (accs[0],)


def _ep_sigmoid(accs, extras):
    return (_sigmoid(accs[0]),)


def _ep_merge(accs, extras):
    return (extras[0].astype(F32) * accs[0] + extras[1].astype(F32) * accs[1],)


def _ep_residual(accs, extras):
    return (extras[0] + accs[0],)


def _ep_swiglu(accs, extras):
    g = accs[0]
    return (g * _sigmoid(g) * accs[1],)


def _rw_prep(p, prev, mu, w0, a0, w2p, a2p, g2p, width):
    pm = p + (prev - p) * mu
    r = pm[:, :width]
    k = pm[:, width:2 * width]
    v = pm[:, 2 * width:3 * width]
    tail = pm[:, 3 * width:3 * width + LORA_TAIL]
    lw = w0 + _dot(jnp.tanh(tail).astype(BF16), w2p)
    w_log = -_softplus(-lw) - 0.5
    logd = -jnp.exp(w_log)
    a_lr = _sigmoid(a0 + _dot(tail.astype(BF16), a2p))
    g = _dot(_sigmoid(tail).astype(BF16), g2p)
    return r, k, v, logd, a_lr, g


def _seg_ones():
    ri = lax.broadcasted_iota(jnp.int32, (LANES, LANES), 0) // RW_HEAD_DIM
    ci = lax.broadcasted_iota(jnp.int32, (LANES, LANES), 1) // RW_HEAD_DIM
    return jnp.where(ri == ci, 1.0, 0.0).astype(BF16)


def _seg_sum(x, ones_bd):
    hi, lo = _split2(x)
    return _dot(hi, ones_bd) + _dot(lo, ones_bd)


def _rw_scan_body(p_ref, mu_ref, w0_ref, a0_ref, kk_ref, ka_ref, rk_ref, lg_ref, lb_ref,
                  w2_ref, a2_ref, g2_ref, y_ref, st_ref, pbuf_ref, state_ref, *, width):
    c = pl.program_id(0)
    ch = SCAN_CHUNK
    n_pair = width // LANES

    @pl.when(c == 0)
    def _():
        pbuf_ref[...] = jnp.zeros_like(pbuf_ref)
        state_ref[...] = jnp.zeros_like(state_ref)

    pbuf_ref[pl.ds(8, ch), :] = p_ref[...]
    p = pbuf_ref[pl.ds(8, ch), :]
    prev = pbuf_ref[pl.ds(7, ch), :]
    r, k, v, logd, a_lr, g = _rw_prep(p, prev, mu_ref[...], w0_ref[...], a0_ref[...],
                                      w2_ref[...], a2_ref[...], g2_ref[...], width)
    pbuf_ref[pl.ds(0, 8), :] = pbuf_ref[pl.ds(ch, 8), :]

    ti = lax.broadcasted_iota(jnp.int32, (ch, ch), 0)
    tj = lax.broadcasted_iota(jnp.int32, (ch, ch), 1)
    ltri = jnp.where(ti >= tj, 1.0, 0.0).astype(BF16)
    d_hi, d_mid, d_lo = _split3(logd)
    lc = _dot(ltri, d_hi) + _dot(ltri, d_mid) + _dot(ltri, d_lo)
    lc_end = lc[ch - 1:ch, :]
    w_t = jnp.exp(lc)
    w_tm1 = jnp.exp(lc - logd)
    w_inv = jnp.exp(-lc)
    w_rem = jnp.exp(lc_end - lc)
    w_end = jnp.exp(lc_end)

    ones_bd = _seg_ones()
    kkr = k * kk_ref[...]
    kf = k * (1.0 + (a_lr - 1.0) * ka_ref[...])
    rkk = r * kf * rk_ref[...]

    lane = lax.broadcasted_iota(jnp.int32, (ch, LANES), 1)
    m0 = jnp.where(lane < RW_HEAD_DIM, 1.0, 0.0)
    m1 = 1.0 - m0

    def stack(z):
        return jnp.concatenate([z * m0, z * m1], axis=0)

    ri = lax.broadcasted_iota(jnp.int32, (2 * ch, 2 * ch), 0) % ch
    ci = lax.broadcasted_iota(jnp.int32, (2 * ch, 2 * ch), 1) % ch
    tri_strict = jnp.where(ri > ci, 1.0, 0.0)
    tri_incl = jnp.where(ri >= ci, 1.0, 0.0)
    er = lax.broadcasted_iota(jnp.int32, (LANES, LANES), 0)
    ec = lax.broadcasted_iota(jnp.int32, (LANES, LANES), 1)
    eye = jnp.where(er == ec, 1.0, 0.0)

    pairs = range(n_pair)
    sls = [slice(gp * LANES, (gp + 1) * LANES) for gp in pairs]
    h2 = 2 * ch

    def rows_of(z_list):
        return jnp.concatenate(z_list, axis=0)

    kkr_g = [kkr[:, sl] for sl in sls]
    ss = _seg_sum(rows_of([z * z for z in kkr_g]), ones_bd)
    kk_g = [kkr_g[gp] / jnp.maximum(jnp.sqrt(ss[gp * ch:(gp + 1) * ch]), 1e-12) for gp in pairs]
    braw_g = [kk_g[gp] * a_lr[:, sls[gp]] for gp in pairs]
    kf_g = [kf[:, sl] for sl in sls]
    a_s = [stack(-kk_g[gp] * w_tm1[:, sls[gp]]).astype(BF16) for gp in pairs]
    r_s = [stack(r[:, sls[gp]] * w_t[:, sls[gp]]).astype(BF16) for gp in pairs]
    v_s = [stack(v[:, sl]).astype(BF16) for sl in sls]
    rhs = [jnp.concatenate([stack(braw_g[gp] * w_inv[:, sls[gp]]), stack(kf_g[gp] * w_inv[:, sls[gp]])],
                           axis=0).astype(BF16) for gp in pairs]
    upd_l = [jnp.concatenate([stack(braw_g[gp] * w_rem[:, sls[gp]]), stack(kf_g[gp] * w_rem[:, sls[gp]])],
                             axis=0).astype(BF16) for gp in pairs]
    gram = [_dot_nt(jnp.concatenate([a_s[gp], r_s[gp]], axis=0), rhs[gp]) for gp in pairs]
    pw = [(gram[gp][:h2, :h2] * tri_strict).astype(BF16) for gp in pairs]
    n_bd = [(gram[gp][:h2, h2:] * tri_strict).astype(BF16) for gp in pairs]
    pb = [(gram[gp][h2:, :h2] * tri_incl).astype(BF16) for gp in pairs]
    pk = [(gram[gp][h2:, h2:] * tri_incl).astype(BF16) for gp in pairs]

    st = [state_ref[gp] for gp in pairs]
    st_b = [z.astype(BF16) for z in st]
    sa = [_dot(jnp.concatenate([a_s[gp], n_bd[gp]], axis=1), jnp.concatenate([st_b[gp], v_s[gp]], axis=0))
          for gp in pairs]
    n_steps = ch.bit_length() - 1
    for s in range(n_steps):
        sa = [sa[gp] + _dot(pw[gp], sa[gp].astype(BF16)) for gp in pairs]
        if s + 1 < n_steps:
            pw = [_dot(pw[gp], pw[gp]).astype(BF16) for gp in pairs]
    sa_b = [z.astype(BF16) for z in sa]
    y_s = [_dot(jnp.concatenate([r_s[gp], pb[gp], pk[gp]], axis=1),
                jnp.concatenate([st_b[gp], sa_b[gp], v_s[gp]], axis=0)) for gp in pairs]
    upd = [_dot_tn(upd_l[gp], jnp.concatenate([sa_b[gp], v_s[gp]], axis=0)) for gp in pairs]
    for gp in pairs:
        w_col = jnp.sum(eye * w_end[:, sls[gp]], axis=1, keepdims=True)
        state_ref[gp] = st[gp] * w_col + upd[gp]

    y = rows_of([y_s[gp][:ch] + y_s[gp][ch:] for gp in pairs])
    mean = _seg_sum(y, ones_bd) * (1.0 / RW_HEAD_DIM)
    d = y - mean
    var = _seg_sum(d * d, ones_bd) * (1.0 / RW_HEAD_DIM)
    yn = d * lax.rsqrt(var + LNX_EPS)
    bonus = _seg_sum(rows_of([rkk[:, sl] for sl in sls]), ones_bd)
    for gp in pairs:
        sl = sls[gp]
        rs = slice(gp * ch, (gp + 1) * ch)
        out = (yn[rs] * lg_ref[:, sl] + lb_ref[:, sl] + bonus[rs] * v[:, sl]) * g[:, sl]
        y_ref[:, sl] = out.astype(y_ref.dtype)

    @pl.when(c == pl.num_programs(0) - 1)
    def _():
        st_ref[...] = state_ref[...]


def _rw_scan_prompt(p_rw, prm, width):
    t, pw = p_rw.shape
    ch = SCAN_CHUNK
    assert t % ch == 0 and width % LANES == 0
    n_pair = width // LANES
    row = lambda n: pl.BlockSpec((1, n), lambda c: (0, 0))
    full = lambda a: pl.BlockSpec(a.shape, lambda c: (0, 0))
    return pl.pallas_call(
        functools.partial(_rw_scan_body, width=width),
        grid=(t // ch,),
        in_specs=[pl.BlockSpec((ch, pw), lambda c: (c, 0)), row(pw)] + [row(width)] * 7
                 + [full(prm["w2p"]), full(prm["a2p"]), full(prm["g2p"])],
        out_specs=[pl.BlockSpec((ch, width), lambda c: (c, 0)),
                   pl.BlockSpec((n_pair, LANES, LANES), lambda c: (0, 0, 0))],
        out_shape=[jax.ShapeDtypeStruct((t, width), BF16),
                   jax.ShapeDtypeStruct((n_pair, LANES, LANES), F32)],
        scratch_shapes=[pltpu.VMEM((ch + 8, pw), F32), pltpu.VMEM((n_pair, LANES, LANES), F32)],
        compiler_params=_params(1),
        name="rwkv_scan",
    )(p_rw, prm["mu"], prm["w0"], prm["a0"], prm["k_k"], prm["k_a"], prm["r_k"], prm["lnx_g"], prm["lnx_b"],
      prm["w2p"], prm["a2p"], prm["g2p"])


def _rw_sample_prep_body(p_ref, prev_ref, mu_ref, w0_ref, a0_ref, w2_ref, a2_ref, g2_ref,
                         r_ref, k_ref, v_ref, d_ref, alr_ref, g_ref, *, width):
    r, k, v, logd, a_lr, g = _rw_prep(p_ref[...], prev_ref[...], mu_ref[...], w0_ref[...], a0_ref[...],
                                      w2_ref[...], a2_ref[...], g2_ref[...], width)
    r_ref[...] = r
    k_ref[...] = k
    v_ref[...] = v
    d_ref[...] = logd
    alr_ref[...] = a_lr
    g_ref[...] = g


def _rw_sample_prep(p_s, prev_s, prm, width):
    b = p_s.shape[0]
    args = (p_s, prev_s, prm["mu"], prm["w0"], prm["a0"], prm["w2p"], prm["a2p"], prm["g2p"])
    return pl.pallas_call(
        functools.partial(_rw_sample_prep_body, width=width),
        grid=(1,),
        in_specs=[pl.BlockSpec(a.shape, lambda i: (0, 0)) for a in args],
        out_specs=[pl.BlockSpec((b, width), lambda i: (0, 0))] * 6,
        out_shape=[jax.ShapeDtypeStruct((b, width), F32)] * 6,
        compiler_params=_params(1),
        name="rwkv_sample_prep",
    )(*args)


def _rw_sample_step_body(s_ref, r_ref, k_ref, v_ref, d_ref, alr_ref, g_ref,
                         kk_ref, ka_ref, rk_ref, lg_ref, lb_ref, so_ref, y_ref):
    s = s_ref[0]
    r, k, v = r_ref[0], k_ref[0], v_ref[0]
    a_lr, g = alr_ref[0], g_ref[0]
    w = jnp.exp(d_ref[0])
    kkr = k * kk_ref[...]
    kk = kkr / jnp.maximum(jnp.sqrt(jnp.sum(kkr * kkr, axis=-1, keepdims=True)), 1e-12)
    kf = k * (1.0 + (a_lr - 1.0) * ka_ref[...])
    a = -kk
    b = kk * a_lr
    n = RW_HEAD_DIM
    eye = (lax.broadcasted_iota(jnp.int32, (n, n), 0) == lax.broadcasted_iota(jnp.int32, (n, n), 1))
    sa = jnp.sum(s * a, axis=-1, keepdims=True)
    v_col = jnp.sum(jnp.where(eye, v, 0.0), axis=-1, keepdims=True)
    s1 = s * w + sa * b + v_col * kf
    so_ref[0] = s1
    y_col = jnp.sum(s1 * r, axis=-1, keepdims=True)
    y = jnp.sum(jnp.where(eye, y_col, 0.0), axis=-2, keepdims=True)
    mean = jnp.mean(y, axis=-1, keepdims=True)
    d = y - mean
    var = jnp.mean(d * d, axis=-1, keepdims=True)
    yn = d * lax.rsqrt(var + LNX_EPS) * lg_ref[...] + lb_ref[...]
    bonus = jnp.sum(r * kf * rk_ref[...], axis=-1, keepdims=True) * v
    y_ref[0] = (yn + bonus) * g


def _rw_sample_step(state, rows, head_prm):
    b, h, n, _ = state.shape
    vec = pl.BlockSpec((1, h, 1, n), lambda i: (i, 0, 0, 0))
    par = pl.BlockSpec((h, 1, n), lambda i: (0, 0, 0))
    mat = pl.BlockSpec((1, h, n, n), lambda i: (i, 0, 0, 0))
    return pl.pallas_call(
        _rw_sample_step_body,
        grid=(b,),
        in_specs=[mat] + [vec] * 6 + [par] * 5,
        out_specs=[mat, vec],
        out_shape=[jax.ShapeDtypeStruct((b, h, n, n), F32), jax.ShapeDtypeStruct((b, h, 1, n), F32)],
        compiler_params=_params(1),
        name="rwkv_sample_step",
    )(state, *rows, *head_prm)


def _top_blocks(gs, n_valid):
    nb = gs.shape[0]
    rowi = lax.broadcasted_iota(jnp.int32, gs.shape, 0)
    rowf = rowi.astype(F32)
    gm = jnp.where(rowi < n_valid, gs, _NEG_INF)
    picked = jnp.zeros(gs.shape, F32)
    order = []
    for _ in range(MB_TOPK):
        mx = jnp.max(gm, axis=0, keepdims=True)
        first = jnp.min(jnp.where(gm == mx, rowf, float(nb)), axis=0, keepdims=True)
        hit = jnp.where(rowf == first, jnp.where(mx > _NEG_INF, 1.0, 0.0), 0.0)
        picked = picked + hit
        gm = jnp.where(hit > 0.0, _NEG_INF, gm)
        order.append(first)
    return picked, order


def _moba_prompt_body(q_ref, k_ref, v_ref, o_ref, kb_ref, vt_ref, vt1_ref, km_ref, bias_ref, *, nb, group):
    i = pl.program_id(1)
    blk, dh = MB_BLOCK, MB_HEAD_DIM

    @pl.when(i == 0)
    def _():
        for n in range(nb):
            kn = k_ref[n * blk:(n + 1) * blk, :]
            u = n % group
            kb_ref[n // group, u * blk:(u + 1) * blk, :] = kn.astype(BF16)
            km_ref[n:n + 1, :] = jnp.mean(kn, axis=0, keepdims=True)
            vn_t = v_ref[n * blk:(n + 1) * blk, :].T.astype(BF16)
            vt_ref[n // group, :, u * blk:(u + 1) * blk] = vn_t
            vt1_ref[n] = vn_t

    q_t = q_ref[...].T
    q_hi, q_lo = _split2(q_t)
    k_hi, k_lo = _split2(km_ref[...])
    gs = _dot(k_hi, q_hi) + _dot(k_hi, q_lo) + _dot(k_lo, q_hi)
    picked, _ = _top_blocks(gs, i)
    bias_ref[...] = jnp.where(picked > 0.0, 0.0, _NEG_INF)

    q_s = (q_t * (dh ** -0.5 * _LOG2E)).astype(BF16)
    own = pl.multiple_of((i % group) * blk, blk)
    s = _dot(kb_ref[i // group, pl.ds(own, blk), :], q_s)
    kr = lax.broadcasted_iota(jnp.int32, (blk, blk), 0)
    qc = lax.broadcasted_iota(jnp.int32, (blk, blk), 1)
    s = jnp.where(kr <= qc, s, _NEG_INF)
    m = jnp.max(s, axis=0, keepdims=True)
    p = jnp.exp2(s - m)
    l = jnp.sum(p, axis=0, keepdims=True)
    acc = _dot(vt1_ref[i], p.astype(BF16))

    def step(jj, carry):
        m, l, acc = carry
        s = _dot(kb_ref[jj], q_s)
        parts = []
        m_new = m
        for u in range(group):
            t_u = s[u * blk:(u + 1) * blk] + bias_ref[pl.ds(jj * group + u, 1), :]
            parts.append(t_u)
            m_new = jnp.maximum(m_new, jnp.max(t_u, axis=0, keepdims=True))
        alpha = jnp.exp2(m - m_new)
        ps = [jnp.exp2(t_u - m_new) for t_u in parts]
        l = alpha * l
        for p_u in ps:
            l = l + jnp.sum(p_u, axis=0, keepdims=True)
        p_all = jnp.concatenate([p_u.astype(BF16) for p_u in ps], axis=0)
        acc = alpha * acc + _dot(vt_ref[jj], p_all)
        return m_new, l, acc

    m, l, acc = lax.fori_loop(0, (i + group - 1) // group, step, (m, l, acc))
    o_ref[...] = (acc * (1.0 / l)).T.astype(o_ref.dtype)


def _moba_prompt(q, k, v, n_heads):
    t = q.shape[0]
    blk, dh = MB_BLOCK, MB_HEAD_DIM
    assert t % blk == 0
    nb = t // blk
    group = 4 if nb % 4 == 0 else 1
    return pl.pallas_call(
        functools.partial(_moba_prompt_body, nb=nb, group=group),
        grid=(n_heads, nb),
        in_specs=[pl.BlockSpec((blk, dh), lambda h, i: (i, h)),
                  pl.BlockSpec((t, dh), lambda h, i: (0, h)),
                  pl.BlockSpec((t, dh), lambda h, i: (0, h))],
        out_specs=pl.BlockSpec((blk, dh), lambda h, i: (i, h)),
        out_shape=jax.ShapeDtypeStruct((t, n_heads * dh), BF16),
        scratch_shapes=[pltpu.VMEM((nb // group, group * blk, dh), BF16),
                        pltpu.VMEM((nb // group, dh, group * blk), BF16),
                        pltpu.VMEM((nb, dh, blk), BF16),
                        pltpu.VMEM((nb, dh), F32), pltpu.VMEM((nb, blk), F32)],
        compiler_params=_params(2),
        name="moba_prompt",
    )(q, k, v)


def _page_mean_body(pt_ref, *refs, pages):
    page_refs, o_ref = refs[:pages], refs[pages]
    n = pl.program_id(1)
    ppb = MB_BLOCK // PAGE_SIZE
    blocks = pages // ppb
    for u in range(blocks):
        tot = jnp.sum(page_refs[u * ppb][0, 0], axis=0)
        for e in range(1, ppb):
            tot = tot + jnp.sum(page_refs[u * ppb + e][0, 0], axis=0)
        o_ref[0, n * blocks + u] = tot * (1.0 / MB_BLOCK)


def _page_means(cache_k, page_table):
    b, n_pages = page_table.shape
    _, _, page, n_heads, dh = cache_k.shape
    ppb = MB_BLOCK // PAGE_SIZE
    assert page == PAGE_SIZE and n_pages % ppb == 0
    nblk = n_pages // ppb
    pages = 8 if n_pages % 8 == 0 else ppb

    def page_map(e):
        return lambda s, n, pt: (0, pt[s, n * pages + e], 0, 0, 0)

    grid_spec = pltpu.PrefetchScalarGridSpec(
        num_scalar_prefetch=1,
        grid=(b, n_pages // pages),
        in_specs=[pl.BlockSpec((1, 1, page, n_heads, dh), page_map(e)) for e in range(pages)],
        out_specs=pl.BlockSpec((1, nblk, n_heads, dh), lambda s, n, pt: (s, 0, 0, 0)),
    )
    return pl.pallas_call(
        functools.partial(_page_mean_body, pages=pages),
        grid_spec=grid_spec,
        out_shape=jax.ShapeDtypeStruct((b, nblk, n_heads, dh), F32),
        compiler_params=_params(2),
        name="moba_page_means",
    )(page_table, *([cache_k] * pages))


def _sample_gate_body(q_ref, km_ref, o_ref):
    km = km_ref[0]
    nblk, n_heads, dh = km.shape
    prod = (km * q_ref[0]).reshape(nblk * n_heads, dh)
    ones = jnp.ones((dh, dh), BF16)
    hi, mid, lo = _split3(prod)
    gs = (_dot(hi, ones) + _dot(mid, ones) + _dot(lo, ones)).reshape(nblk, n_heads, dh)
    _, order = _top_blocks(gs, nblk)
    order.append(jnp.zeros((8 - MB_TOPK, n_heads, dh), F32))
    o_ref[0] = jnp.concatenate(order, axis=0).astype(jnp.int32)


def _sample_gate(q_s3, kmean):
    b, nblk, n_heads, dh = kmean.shape
    return pl.pallas_call(
        _sample_gate_body,
        grid=(b,),
        in_specs=[pl.BlockSpec((1, n_heads, dh), lambda s: (s, 0, 0)),
                  pl.BlockSpec((1, nblk, n_heads, dh), lambda s: (s, 0, 0, 0))],
        out_specs=pl.BlockSpec((1, 8, n_heads, dh), lambda s: (s, 0, 0, 0)),
        out_shape=jax.ShapeDtypeStruct((b, 8, n_heads, dh), jnp.int32),
        compiler_params=_params(1),
        name="moba_sample_gate",
    )(q_s3, kmean)


def _sample_attn_body(pt_ref, top_ref, q_ref, kn_ref, vn_ref, ck_ref, cv_ref, o_ref, kbuf, vbuf, sem, *, n_heads):
    s = pl.program_id(0)
    ppb = MB_BLOCK // PAGE_SIZE
    n_pg = MB_TOPK * ppb
    dh = MB_HEAD_DIM

    def copies(seq, slot):
        out = []
        for h in range(n_heads):
            for t in range(MB_TOPK):
                blk_id = top_ref[seq, h * MB_TOPK + t]
                for e in range(ppb):
                    page = pt_ref[seq, ppb * blk_id + e]
                    out.append(pltpu.make_async_copy(ck_ref.at[0, page, :, h, :], kbuf.at[slot, h, t * ppb + e],
                                                     sem.at[0, slot]))
                    out.append(pltpu.make_async_copy(cv_ref.at[0, page, :, h, :], vbuf.at[slot, h, t * ppb + e],
                                                     sem.at[1, slot]))
        return out

    slot = s % 2

    @pl.when(s == 0)
    def _():
        for cp in copies(0, 0):
            cp.start()

    @pl.when(s + 1 < pl.num_programs(0))
    def _():
        for cp in copies(s + 1, 1 - slot):
            cp.start()

    for cp in copies(s, slot):
        cp.wait()

    scale = dh ** -0.5
    for h in range(n_heads):
        q = q_ref[0, h:h + 1, :] * scale
        keys = kbuf[slot, h].reshape(n_pg * PAGE_SIZE, dh).astype(BF16)
        vals = vbuf[slot, h].reshape(n_pg * PAGE_SIZE, dh).astype(BF16)
        sc = _dot_nt(jnp.broadcast_to(q, (8, dh)).astype(BF16), keys)[0:1, :]
        s_self = jnp.sum(q * kn_ref[0, h:h + 1, :], axis=-1, keepdims=True)
        m = jnp.maximum(jnp.max(sc, axis=-1, keepdims=True), s_self)
        p = jnp.exp(sc - m)
        p_self = jnp.exp(s_self - m)
        l = jnp.sum(p, axis=-1, keepdims=True) + p_self
        pv = _dot(jnp.broadcast_to(p, (8, p.shape[1])).astype(BF16), vals)[0:1, :]
        o_ref[0, h:h + 1, :] = (pv + vn_ref[0, h:h + 1, :] * p_self) * (1.0 / l)


def _sample_attn(q_s3, k_s3, v_s3, cache_k, cache_v, page_table, top, n_heads):
    b = q_s3.shape[0]
    dh = MB_HEAD_DIM
    n_pg = MB_TOPK * (MB_BLOCK // PAGE_SIZE)
    row = pl.BlockSpec((1, n_heads, dh), lambda s, pt, tp: (s, 0, 0))
    hbm = pl.BlockSpec(memory_space=pl.ANY)
    grid_spec = pltpu.PrefetchScalarGridSpec(
        num_scalar_prefetch=2,
        grid=(b,),
        in_specs=[row, row, row, hbm, hbm],
        out_specs=row,
        scratch_shapes=[pltpu.VMEM((2, n_heads, n_pg, PAGE_SIZE, dh), F32),
                        pltpu.VMEM((2, n_heads, n_pg, PAGE_SIZE, dh), F32),
                        pltpu.SemaphoreType.DMA((2, 2))],
    )
    return pl.pallas_call(
        functools.partial(_sample_attn_body, n_heads=n_heads),
        grid_spec=grid_spec,
        out_shape=jax.ShapeDtypeStruct((b, n_heads, dh), F32),
        compiler_params=_params(1),
        name="moba_sample_attn",
    )(page_table, top, q_s3, k_s3, v_s3, cache_k, cache_v)


def _layer(xp, xs, cache_k, cache_v, wkv_s, shift_s, page_table, norm_mix, w_in, rwkv_mu, w0, w2, a0, a2, g2,
           k_k, k_a, r_k, lnx_g, lnx_b, up_rw, up_mb, w_o, norm_ffn, w_gate, w_up, w_down):
    t, d_model = xp.shape
    b = xs.shape[0]
    n_rw_heads = r_k.shape[0]
    rw_width = n_rw_heads * RW_HEAD_DIM
    n_mb_heads = cache_k.shape[3]
    mb_width = n_mb_heads * MB_HEAD_DIM
    rw_proj = rwkv_mu.shape[0]
    lora = DECAY_LORA + ICLR_LORA + GATE_LORA
    assert rw_proj == 3 * rw_width + lora
    rw_pad = 3 * rw_width + LORA_TAIL
    rw_col0 = 3 * mb_width
    gate_col0 = rw_col0 + rw_proj
    d_ff = w_gate.shape[1]

    xn_p = _rmsnorm(xp, norm_mix, BF16)
    xn_s = _rmsnorm(xs, norm_mix, BF16)

    def proj(name, col0, n_out, tn, ep=_ep_identity, w=w_in, dt=F32):
        (o_p,), (o_s,) = _fused_matmul(name, [xn_p], [xn_s], [w], [col0], n_out, _pick_tile(n_out, tn), 1024,
                                       [], ep, [dt])
        return o_p, o_s

    q_p, q_s = proj("proj_q", 0, mb_width, 512)
    k_p, k_s = proj("proj_k", mb_width, mb_width, 512)
    v_p, v_s = proj("proj_v", 2 * mb_width, mb_width, 512)
    p_rw_p, p_rw_s = proj("proj_rw", rw_col0, rw_pad, 512)
    w_gates = w_in[:, gate_col0:].astype(BF16)
    sg_p, sg_s = proj("proj_gates", 0, 2 * d_model, 512, ep=_ep_sigmoid, w=w_gates, dt=BF16)

    pad_row = lambda a: jnp.pad(a, (0, rw_pad - rw_proj)).reshape(1, rw_pad)
    lora_pad = lambda a, r0: jnp.pad(a, ((r0, LORA_TAIL - r0 - a.shape[0]), (0, 0))).astype(BF16)
    prm = dict(
        mu=pad_row(rwkv_mu), w0=w0.reshape(1, rw_width), a0=a0.reshape(1, rw_width),
        k_k=k_k.reshape(1, rw_width), k_a=k_a.reshape(1, rw_width), r_k=r_k.reshape(1, rw_width),
        lnx_g=lnx_g.reshape(1, rw_width), lnx_b=lnx_b.reshape(1, rw_width),
        w2p=lora_pad(w2, 0), a2p=lora_pad(a2, DECAY_LORA), g2p=lora_pad(g2, DECAY_LORA + ICLR_LORA))
    y_rw_p, st_pairs = _rw_scan_prompt(p_rw_p, prm, rw_width)
    n_pair = rw_width // LANES
    st5 = st_pairs.reshape(n_pair, 2, RW_HEAD_DIM, 2, RW_HEAD_DIM)
    wkv_p = jnp.stack([st5[:, 0, :, 0, :], st5[:, 1, :, 1, :]], axis=1)
    wkv_p = jnp.swapaxes(wkv_p, -1, -2).reshape(n_rw_heads, RW_HEAD_DIM, RW_HEAD_DIM)
    shift_p = p_rw_p[t - 1, :rw_proj]

    prev_s = jnp.pad(shift_s, ((0, 0), (0, rw_pad - rw_proj)))
    rows = _rw_sample_prep(p_rw_s, prev_s, prm, rw_width)
    rows = [a.reshape(b, n_rw_heads, 1, RW_HEAD_DIM) for a in rows]
    head_prm = [a.reshape(n_rw_heads, 1, RW_HEAD_DIM) for a in (k_k, k_a, r_k, lnx_g, lnx_b)]
    wkv_s_new, y_rw_s = _rw_sample_step(wkv_s, rows, head_prm)
    y_rw_s = y_rw_s.reshape(b, rw_width).astype(BF16)
    shift_s_new = p_rw_s[:, :rw_proj]

    y_mb_p = _moba_prompt(q_p, k_p, v_p, n_mb_heads)
    kmean_s = _page_means(cache_k, page_table)
    q_s3 = q_s.reshape(b, n_mb_heads, MB_HEAD_DIM)
    top = _sample_gate(q_s3, kmean_s)
    top = jnp.swapaxes(top[:, :MB_TOPK, :, 0], 1, 2).reshape(b, n_mb_heads * MB_TOPK)
    y_mb_s = _sample_attn(q_s3, k_s.reshape(b, n_mb_heads, MB_HEAD_DIM), v_s.reshape(b, n_mb_heads, MB_HEAD_DIM),
                          cache_k, cache_v, page_table, top, n_mb_heads)
    y_mb_s = y_mb_s.reshape(b, mb_width).astype(BF16)

    (mg_p,), (mg_s,) = _fused_matmul("merge", [y_rw_p, y_mb_p], [y_rw_s, y_mb_s], [up_rw, up_mb], [0, 0],
                                     d_model, _pick_tile(d_model, 512), 1024,
                                     [(sg_p, sg_s, 0), (sg_p, sg_s, d_model)], _ep_merge, [BF16])
    (x2_p,), (x2_s,) = _fused_matmul("out_proj", [mg_p], [mg_s], [w_o], [0], d_model, _pick_tile(d_model, 512), 1024,
                                     [(xp, xs, 0)], _ep_residual, [F32])
    xn2_p = _rmsnorm(x2_p, norm_ffn, BF16)
    xn2_s = _rmsnorm(x2_s, norm_ffn, BF16)
    (h_p,), (h_s,) = _fused_matmul("ffn_up", [xn2_p], [xn2_s], [w_gate, w_up], [0, 0],
                                   d_ff, _pick_tile(d_ff, 256), 1024, [], _ep_swiglu, [BF16], x_of=[0, 0])
    (x3_p,), (x3_s,) = _fused_matmul("ffn_down", [h_p], [h_s], [w_down.astype(BF16)], [0], d_model,
                                     _pick_tile(d_model, 512), 512, [(x2_p, x2_s, 0)], _ep_residual, [F32])
    return (x3_p, x3_s, k_p, v_p, wkv_p, shift_p, k_s, v_s, wkv_s_new, shift_s_new)


def kernel(x_prompt, x_sample, cache_k, cache_v, state_wkv, state_shift, page_table, norm_mix, w_in, rwkv_mu, w0, w2, a0, a2, g2, k_k, k_a, r_k, lnx_g, lnx_b, up_rw, up_mb, w_o, norm_ffn, w_gate, w_up, w_down, norm_final):
    depth = w_in.shape[0]
    assert depth == 1 and x_prompt.shape[0] == 1 and x_sample.shape[1] == 1
    t, d_model = x_prompt.shape[1], x_prompt.shape[2]
    b = x_sample.shape[0]
    n_mb_heads = cache_k.shape[3]
    n_rw_heads = r_k.shape[1]
    out = _layer(x_prompt[0], x_sample[:, 0], cache_k, cache_v, state_wkv[0], state_shift[0], page_table,
                 norm_mix[0], w_in[0], rwkv_mu[0], w0[0], w2[0], a0[0], a2[0], g2[0], k_k[0], k_a[0], r_k[0],
                 lnx_g[0], lnx_b[0], up_rw[0], up_mb[0], w_o[0], norm_ffn[0], w_gate[0], w_up[0], w_down[0])
    x3_p, x3_s, k_p, v_p, wkv_p, shift_p, k_s, v_s, wkv_s, shift_s = out
    y_p = _rmsnorm(x3_p, norm_final, F32)
    y_s = _rmsnorm(x3_s, norm_final, F32)
    rw_proj = rwkv_mu.shape[1]
    return (y_p.reshape(1, t, d_model), y_s.reshape(b, 1, d_model),
            k_p.reshape(1, 1, t, n_mb_heads, MB_HEAD_DIM), v_p.reshape(1, 1, t, n_mb_heads, MB_HEAD_DIM),
            wkv_p.reshape(1, 1, n_rw_heads, RW_HEAD_DIM, RW_HEAD_DIM), shift_p.reshape(1, 1, rw_proj),
            k_s.reshape(1, b, 1, n_mb_heads, MB_HEAD_DIM), v_s.reshape(1, b, 1, n_mb_heads, MB_HEAD_DIM),
            wkv_s.reshape(1, b, n_rw_heads, RW_HEAD_DIM, RW_HEAD_DIM), shift_s.reshape(1, b, rw_proj))
```

```python
import functools

import jax
import jax.numpy as jnp
from jax import lax
from jax.experimental import pallas as pl
from jax.experimental.pallas import tpu as pltpu

F32 = jnp.float32
BF16 = jnp.bfloat16

RMS_EPS = 1e-6
LNX_EPS = 64e-5
MB_BLOCK = 256
MB_TOPK = 3
MB_HEAD_DIM = 128
PAGE_SIZE = 128
RW_HEAD_DIM = 64
DECAY_LORA = 96
ICLR_LORA = 96
GATE_LORA = 256
LORA_TAIL = 512
SCAN_CHUNK = 64
LANES = 128
VMEM_LIMIT = 56 * 1024 * 1024

_NEG_INF = float("-inf")
_LOG2E = 1.4426950408889634


def _params(n_axes):
    return pltpu.CompilerParams(dimension_semantics=("arbitrary",) * n_axes,
                                vmem_limit_bytes=VMEM_LIMIT)


def _dot(a, b):
    return jnp.dot(a, b, preferred_element_type=F32)


def _dot_nt(a, b):
    return lax.dot_general(a, b, (((1,), (1,)), ((), ())), preferred_element_type=F32)


def _dot_tn(a, b):
    return lax.dot_general(a, b, (((0,), (0,)), ((), ())), preferred_element_type=F32)


def _split2(x):
    hi = x.astype(BF16)
    lo = (x - hi.astype(F32)).astype(BF16)
    return hi, lo


def _split3(x):
    hi = x.astype(BF16)
    r1 = x - hi.astype(F32)
    mid = r1.astype(BF16)
    lo = (r1 - mid.astype(F32)).astype(BF16)
    return hi, mid, lo


def _sigmoid(x):
    return 1.0 / (1.0 + jnp.exp(-x))


def _softplus(x):
    return jnp.maximum(x, 0.0) + jnp.log1p(jnp.exp(-jnp.abs(x)))


def _rmsnorm_body(x_ref, g_ref, o_ref):
    x = x_ref[...]
    ms = jnp.mean(x * x, axis=-1, keepdims=True)
    o_ref[...] = (x * lax.rsqrt(ms + RMS_EPS) * g_ref[...]).astype(o_ref.dtype)


def _rmsnorm(x, g, out_dtype, tm=256):
    m, d = x.shape
    tm = min(tm, m)
    assert m % tm == 0
    return pl.pallas_call(
        _rmsnorm_body,
        grid=(m // tm,),
        in_specs=[pl.BlockSpec((tm, d), lambda i: (i, 0)), pl.BlockSpec((1, d), lambda i: (0, 0))],
        out_specs=pl.BlockSpec((tm, d), lambda i: (i, 0)),
        out_shape=jax.ShapeDtypeStruct((m, d), out_dtype),
        compiler_params=_params(1),
        name="rmsnorm",
    )(x, g.reshape(1, d))


def _pick_tile(n, pref):
    t = pref
    while n % t:
        t //= 2
    assert t >= LANES
    return t


def _fused_matmul(name, xp, xs, ws, w_col0, n_out, tn, tm, extras, epilogue, out_dtypes, x_of=None):
    nd, ne, no, nx = len(ws), len(extras), len(out_dtypes), len(xp)
    x_of = list(range(nd)) if x_of is None else x_of
    mp, ms = xp[0].shape[0], xs[0].shape[0]
    tm = min(tm, mp)
    assert mp % tm == 0 and n_out % tn == 0
    nj, ni = n_out // tn, mp // tm
    for c in w_col0:
        assert c % tn == 0
    cast = [w.dtype != BF16 for w in ws]
    n_cast = sum(cast)
    cast_slot = [sum(cast[:k]) for k in range(nd)]

    def body(*refs):
        xp_r = refs[:nx]
        xs_r = refs[nx:2 * nx]
        w_r = refs[2 * nx:2 * nx + nd]
        base = 2 * nx + nd
        ep_r = refs[base:base + ne]
        es_r = refs[base + ne:base + 2 * ne]
        base += 2 * ne
        op_r = refs[base:base + no]
        os_r = refs[base + no:base + 2 * no]
        wb_r = refs[base + 2 * no:]
        i = pl.program_id(1)

        def weight(k):
            return wb_r[cast_slot[k]][...] if cast[k] else w_r[k][...]

        @pl.when(i == 0)
        def _():
            for k in range(nd):
                if cast[k]:
                    wb_r[cast_slot[k]][...] = w_r[k][...].astype(BF16)
            accs = [_dot(xs_r[x_of[k]][...], weight(k)) for k in range(nd)]
            outs = epilogue(accs, [e[...] for e in es_r])
            for o, val in zip(os_r, outs):
                o[...] = val.astype(o.dtype)

        accs = [_dot(xp_r[x_of[k]][...], weight(k)) for k in range(nd)]
        outs = epilogue(accs, [e[...] for e in ep_r])
        for o, val in zip(op_r, outs):
            o[...] = val.astype(o.dtype)

    in_specs = []
    for k in range(nx):
        kk = xp[k].shape[1]
        in_specs.append(pl.BlockSpec((tm, kk), lambda j, i: (i, 0)))
    for k in range(nx):
        kk = xs[k].shape[1]
        in_specs.append(pl.BlockSpec((ms, kk), lambda j, i: (0, 0)))
    for k in range(nd):
        kk = ws[k].shape[0]
        in_specs.append(pl.BlockSpec((kk, tn), functools.partial(lambda j, i, c: (0, j + c), c=w_col0[k] // tn)))
    for (_, _, c0) in extras:
        assert c0 % tn == 0
        in_specs.append(pl.BlockSpec((tm, tn), functools.partial(lambda j, i, c: (i, j + c), c=c0 // tn)))
    for (_, _, c0) in extras:
        in_specs.append(pl.BlockSpec((ms, tn), functools.partial(lambda j, i, c: (0, j + c), c=c0 // tn)))
    out_specs = [pl.BlockSpec((tm, tn), lambda j, i: (i, j)) for _ in range(no)]
    out_specs += [pl.BlockSpec((ms, tn), lambda j, i: (0, j)) for _ in range(no)]
    out_shape = [jax.ShapeDtypeStruct((mp, n_out), dt) for dt in out_dtypes]
    out_shape += [jax.ShapeDtypeStruct((ms, n_out), dt) for dt in out_dtypes]
    scratch = [pltpu.VMEM((ws[k].shape[0], tn), BF16) for k in range(nd) if cast[k]]
    assert len(scratch) == n_cast
    args = list(xp) + list(xs) + list(ws) + [e[0] for e in extras] + [e[1] for e in extras]
    outs = pl.pallas_call(
        body,
        grid=(nj, ni),
        in_specs=in_specs,
        out_specs=out_specs,
        out_shape=out_shape,
        scratch_shapes=scratch,
        compiler_params=_params(2),
        name=name,
    )(*args)
    return outs[:no], outs[no:]


def _realign_body(w_ref, o_ref, buf_ref, sem_ref, *, rows, col0, width, shift):
    i = pl.program_id(0)

    def fetch(step, slot):
        return pltpu.make_async_copy(w_ref.at[pl.ds(step * rows, rows), pl.ds(col0, width)],
                                     buf_ref.at[slot], sem_ref.at[slot])

    slot = i % 2

    @pl.when(i == 0)
    def _():
        fetch(0, 0).start()

    @pl.when(i + 1 < pl.num_programs(0))
    def _():
        fetch(i + 1, 1 - slot).start()

    fetch(i, slot).wait()
    o_ref[...] = buf_ref[slot][:, shift:shift + o_ref.shape[1]].astype(o_ref.dtype)


def _realign_cols(w, col0, rows=256):
    k, n = w.shape
    base = (col0 // LANES) * LANES
    rows = min(rows, k)
    assert k % rows == 0 and (n - col0) % LANES == 0
    return pl.pallas_call(
        functools.partial(_realign_body, rows=rows, col0=base, width=n - base, shift=col0 - base),
        grid=(k // rows,),
        in_specs=[pl.BlockSpec(memory_space=pl.ANY)],
        out_specs=pl.BlockSpec((rows, n - col0), lambda i: (i, 0)),
        out_shape=jax.ShapeDtypeStruct((k, n - col0), BF16),
        scratch_shapes=[pltpu.VMEM((2, rows, n - base), w.dtype), pltpu.SemaphoreType.DMA((2,))],
        compiler_params=_params(1),
        name="realign_gate_weights",
    )(w)


def _ep_identity(accs, extras):
    return (accs[0],)


def _ep_sigmoid(accs, extras):
    return (_sigmoid(accs[0]),)


def _ep_merge(accs, extras):
    return (extras[0].astype(F32) * accs[0] + extras[1].astype(F32) * accs[1],)


def _ep_residual(accs, extras):
    return (extras[0] + accs[0],)


def _ep_swiglu(accs, extras):
    g = accs[0]
    return (g * _sigmoid(g) * accs[1],)


def _rw_prep(p, prev, mu, w0, a0, w2p, a2p, g2p, width):
    pm = p + (prev - p) * mu
    r = pm[:, :width]
    k = pm[:, width:2 * width]
    v = pm[:, 2 * width:3 * width]
    tail = pm[:, 3 * width:3 * width + LORA_TAIL]
    lw = w0 + _dot(jnp.tanh(tail).astype(BF16), w2p)
    w_log = -_softplus(-lw) - 0.5
    logd = -jnp.exp(w_log)
    a_lr = _sigmoid(a0 + _dot(tail.astype(BF16), a2p))
    g = _dot(_sigmoid(tail).astype(BF16), g2p)
    return r, k, v, logd, a_lr, g


def _seg_ones():
    ri = lax.broadcasted_iota(jnp.int32, (LANES, LANES), 0) // RW_HEAD_DIM
    ci = lax.broadcasted_iota(jnp.int32, (LANES, LANES), 1) // RW_HEAD_DIM
    return jnp.where(ri == ci, 1.0, 0.0).astype(BF16)


def _seg_sum(x, ones_bd):
    return _dot(x.astype(BF16), ones_bd)


def _rw_scan_body(p_ref, mu_ref, w0_ref, a0_ref, kk_ref, ka_ref, rk_ref, lg_ref, lb_ref,
                  w2_ref, a2_ref, g2_ref, y_ref, st_ref, pbuf_ref, state_ref, *, width):
    c = pl.program_id(0)
    ch = SCAN_CHUNK
    n_pair = width // LANES

    @pl.when(c == 0)
    def _():
        pbuf_ref[...] = jnp.zeros_like(pbuf_ref)
        state_ref[...] = jnp.zeros_like(state_ref)

    pbuf_ref[pl.ds(8, ch), :] = p_ref[...]
    p = pbuf_ref[pl.ds(8, ch), :]
    prev = pbuf_ref[pl.ds(7, ch), :]
    r, k, v, logd, a_lr, g = _rw_prep(p, prev, mu_ref[...], w0_ref[...], a0_ref[...],
                                      w2_ref[...], a2_ref[...], g2_ref[...], width)
    pbuf_ref[pl.ds(0, 8), :] = pbuf_ref[pl.ds(ch, 8), :]

    ti = lax.broadcasted_iota(jnp.int32, (ch, ch), 0)
    tj = lax.broadcasted_iota(jnp.int32, (ch, ch), 1)
    ltri = jnp.where(ti >= tj, 1.0, 0.0).astype(BF16)
    d_hi, d_mid, d_lo = _split3(logd)
    lc = _dot(ltri, d_hi) + _dot(ltri, d_mid) + _dot(ltri, d_lo)
    lc_end = lc[ch - 1:ch, :]
    w_t = jnp.exp(lc)
    w_tm1 = jnp.exp(lc - logd)
    w_inv = jnp.exp(-lc)
    w_rem = jnp.exp(lc_end - lc)
    w_end = jnp.exp(lc_end)

    ones_bd = _seg_ones()
    kkr = k * kk_ref[...]
    kf = k * (1.0 + (a_lr - 1.0) * ka_ref[...])
    rkk = r * kf * rk_ref[...]

    lane = lax.broadcasted_iota(jnp.int32, (ch, LANES), 1)
    m0 = jnp.where(lane < RW_HEAD_DIM, 1.0, 0.0)
    m1 = 1.0 - m0

    def stack(z):
        return jnp.concatenate([z * m0, z * m1], axis=0)

    ri = lax.broadcasted_iota(jnp.int32, (2 * ch, 2 * ch), 0) % ch
    ci = lax.broadcasted_iota(jnp.int32, (2 * ch, 2 * ch), 1) % ch
    tri_strict = jnp.where(ri > ci, 1.0, 0.0)
    tri_incl = jnp.where(ri >= ci, 1.0, 0.0)
    er = lax.broadcasted_iota(jnp.int32, (LANES, LANES), 0)
    ec = lax.broadcasted_iota(jnp.int32, (LANES, LANES), 1)
    eye = jnp.where(er == ec, 1.0, 0.0)

    pairs = range(n_pair)
    sls = [slice(gp * LANES, (gp + 1) * LANES) for gp in pairs]
    h2 = 2 * ch

    def rows_of(z_list):
        return jnp.concatenate(z_list, axis=0)

    kkr_g = [kkr[:, sl] for sl in sls]
    ss = _seg_sum(rows_of([z * z for z in kkr_g]), ones_bd)
    kk_g = [kkr_g[gp] / jnp.maximum(jnp.sqrt(ss[gp * ch:(gp + 1) * ch]), 1e-12) for gp in pairs]
    braw_g = [kk_g[gp] * a_lr[:, sls[gp]] for gp in pairs]
    kf_g = [kf[:, sl] for sl in sls]
    a_s = [stack(-kk_g[gp] * w_tm1[:, sls[gp]]).astype(BF16) for gp in pairs]
    r_s = [stack(r[:, sls[gp]] * w_t[:, sls[gp]]).astype(BF16) for gp in pairs]
    v_s = [stack(v[:, sl]).astype(BF16) for sl in sls]
    rhs = [jnp.concatenate([stack(braw_g[gp] * w_inv[:, sls[gp]]), stack(kf_g[gp] * w_inv[:, sls[gp]])],
                           axis=0).astype(BF16) for gp in pairs]
    upd_l = [jnp.concatenate([stack(braw_g[gp] * w_rem[:, sls[gp]]), stack(kf_g[gp] * w_rem[:, sls[gp]])],
                             axis=0).astype(BF16) for gp in pairs]
    gram = [_dot_nt(jnp.concatenate([a_s[gp], r_s[gp]], axis=0), rhs[gp]) for gp in pairs]
    pw = [(gram[gp][:h2, :h2] * tri_strict).astype(BF16) for gp in pairs]
    n_bd = [(gram[gp][:h2, h2:] * tri_strict).astype(BF16) for gp in pairs]
    pb = [(gram[gp][h2:, :h2] * tri_incl).astype(BF16) for gp in pairs]
    pk = [(gram[gp][h2:, h2:] * tri_incl).astype(BF16) for gp in pairs]

    st = [state_ref[gp] for gp in pairs]
    st_b = [z.astype(BF16) for z in st]
    sa = [_dot(jnp.concatenate([a_s[gp], n_bd[gp]], axis=1), jnp.concatenate([st_b[gp], v_s[gp]], axis=0))
          for gp in pairs]
    n_steps = ch.bit_length() - 1
    for s in range(n_steps):
        sa = [sa[gp] + _dot(pw[gp], sa[gp].astype(BF16)) for gp in pairs]
        if s + 1 < n_steps:
            pw = [_dot(pw[gp], pw[gp]).astype(BF16) for gp in pairs]
    sa_b = [z.astype(BF16) for z in sa]
    y_s = [_dot(jnp.concatenate([r_s[gp], pb[gp], pk[gp]], axis=1),
                jnp.concatenate([st_b[gp], sa_b[gp], v_s[gp]], axis=0)) for gp in pairs]
    upd = [_dot_tn(upd_l[gp], jnp.concatenate([sa_b[gp], v_s[gp]], axis=0)) for gp in pairs]
    for gp in pairs:
        w_col = jnp.sum(eye * w_end[:, sls[gp]], axis=1, keepdims=True)
        state_ref[gp] = st[gp] * w_col + upd[gp]

    y = rows_of([y_s[gp][:ch] + y_s[gp][ch:] for gp in pairs])
    mean = _seg_sum(y, ones_bd) * (1.0 / RW_HEAD_DIM)
    d = y - mean
    var = _seg_sum(d * d, ones_bd) * (1.0 / RW_HEAD_DIM)
    yn = d * lax.rsqrt(var + LNX_EPS)
    bonus = _seg_sum(rows_of([rkk[:, sl] for sl in sls]), ones_bd)
    for gp in pairs:
        sl = sls[gp]
        rs = slice(gp * ch, (gp + 1) * ch)
        out = (yn[rs] * lg_ref[:, sl] + lb_ref[:, sl] + bonus[rs] * v[:, sl]) * g[:, sl]
        y_ref[:, sl] = out.astype(y_ref.dtype)

    @pl.when(c == pl.num_programs(0) - 1)
    def _():
        st_ref[...] = state_ref[...]


def _rw_scan_prompt(p_rw, prm, width):
    t, pw = p_rw.shape
    ch = SCAN_CHUNK
    assert t % ch == 0 and width % LANES == 0
    n_pair = width // LANES
    row = lambda n: pl.BlockSpec((1, n), lambda c: (0, 0))
    full = lambda a: pl.BlockSpec(a.shape, lambda c: (0, 0))
    return pl.pallas_call(
        functools.partial(_rw_scan_body, width=width),
        grid=(t // ch,),
        in_specs=[pl.BlockSpec((ch, pw), lambda c: (c, 0)), row(pw)] + [row(width)] * 7
                 + [full(prm["w2p"]), full(prm["a2p"]), full(prm["g2p"])],
        out_specs=[pl.BlockSpec((ch, width), lambda c: (c, 0)),
                   pl.BlockSpec((n_pair, LANES, LANES), lambda c: (0, 0, 0))],
        out_shape=[jax.ShapeDtypeStruct((t, width), BF16),
                   jax.ShapeDtypeStruct((n_pair, LANES, LANES), F32)],
        scratch_shapes=[pltpu.VMEM((ch + 8, pw), F32), pltpu.VMEM((n_pair, LANES, LANES), F32)],
        compiler_params=_params(1),
        name="rwkv_scan",
    )(p_rw, prm["mu"], prm["w0"], prm["a0"], prm["k_k"], prm["k_a"], prm["r_k"], prm["lnx_g"], prm["lnx_b"],
      prm["w2p"], prm["a2p"], prm["g2p"])


def _rw_sample_prep_body(p_ref, prev_ref, mu_ref, w0_ref, a0_ref, w2_ref, a2_ref, g2_ref,
                         r_ref, k_ref, v_ref, d_ref, alr_ref, g_ref, *, width):
    r, k, v, logd, a_lr, g = _rw_prep(p_ref[...], prev_ref[...], mu_ref[...], w0_ref[...], a0_ref[...],
                                      w2_ref[...], a2_ref[...], g2_ref[...], width)
    r_ref[...] = r
    k_ref[...] = k
    v_ref[...] = v
    d_ref[...] = logd
    alr_ref[...] = a_lr
    g_ref[...] = g


def _rw_sample_prep(p_s, prev_s, prm, width):
    b = p_s.shape[0]
    args = (p_s, prev_s, prm["mu"], prm["w0"], prm["a0"], prm["w2p"], prm["a2p"], prm["g2p"])
    return pl.pallas_call(
        functools.partial(_rw_sample_prep_body, width=width),
        grid=(1,),
        in_specs=[pl.BlockSpec(a.shape, lambda i: (0, 0)) for a in args],
        out_specs=[pl.BlockSpec((b, width), lambda i: (0, 0))] * 6,
        out_shape=[jax.ShapeDtypeStruct((b, width), F32)] * 6,
        compiler_params=_params(1),
        name="rwkv_sample_prep",
    )(*args)


def _rw_sample_step_body(s_ref, r_ref, k_ref, v_ref, d_ref, alr_ref, g_ref,
                         kk_ref, ka_ref, rk_ref, lg_ref, lb_ref, so_ref, y_ref):
    s = s_ref[0]
    r, k, v = r_ref[0], k_ref[0], v_ref[0]
    a_lr, g = alr_ref[0], g_ref[0]
    w = jnp.exp(d_ref[0])
    kkr = k * kk_ref[...]
    kk = kkr / jnp.maximum(jnp.sqrt(jnp.sum(kkr * kkr, axis=-1, keepdims=True)), 1e-12)
    kf = k * (1.0 + (a_lr - 1.0) * ka_ref[...])
    a = -kk
    b = kk * a_lr
    n = RW_HEAD_DIM
    eye = (lax.broadcasted_iota(jnp.int32, (n, n), 0) == lax.broadcasted_iota(jnp.int32, (n, n), 1))
    sa = jnp.sum(s * a, axis=-1, keepdims=True)
    v_col = jnp.sum(jnp.where(eye, v, 0.0), axis=-1, keepdims=True)
    s1 = s * w + sa * b + v_col * kf
    so_ref[0] = s1
    y_col = jnp.sum(s1 * r, axis=-1, keepdims=True)
    y = jnp.sum(jnp.where(eye, y_col, 0.0), axis=-2, keepdims=True)
    mean = jnp.mean(y, axis=-1, keepdims=True)
    d = y - mean
    var = jnp.mean(d * d, axis=-1, keepdims=True)
    yn = d * lax.rsqrt(var + LNX_EPS) * lg_ref[...] + lb_ref[...]
    bonus = jnp.sum(r * kf * rk_ref[...], axis=-1, keepdims=True) * v
    y_ref[0] = (yn + bonus) * g


def _rw_sample_step(state, rows, head_prm):
    b, h, n, _ = state.shape
    vec = pl.BlockSpec((1, h, 1, n), lambda i: (i, 0, 0, 0))
    par = pl.BlockSpec((h, 1, n), lambda i: (0, 0, 0))
    mat = pl.BlockSpec((1, h, n, n), lambda i: (i, 0, 0, 0))
    return pl.pallas_call(
        _rw_sample_step_body,
        grid=(b,),
        in_specs=[mat] + [vec] * 6 + [par] * 5,
        out_specs=[mat, vec],
        out_shape=[jax.ShapeDtypeStruct((b, h, n, n), F32), jax.ShapeDtypeStruct((b, h, 1, n), F32)],
        compiler_params=_params(1),
        name="rwkv_sample_step",
    )(state, *rows, *head_prm)


def _top_blocks(gs, n_valid):
    nb = gs.shape[0]
    rowi = lax.broadcasted_iota(jnp.int32, gs.shape, 0)
    rowf = rowi.astype(F32)
    gm = jnp.where(rowi < n_valid, gs, _NEG_INF)
    picked = jnp.zeros(gs.shape, F32)
    order = []
    for _ in range(MB_TOPK):
        mx = jnp.max(gm, axis=0, keepdims=True)
        first = jnp.min(jnp.where(gm == mx, rowf, float(nb)), axis=0, keepdims=True)
        hit = jnp.where(rowf == first, jnp.where(mx > _NEG_INF, 1.0, 0.0), 0.0)
        picked = picked + hit
        gm = jnp.where(hit > 0.0, _NEG_INF, gm)
        order.append(first)
    return picked, order


def _moba_prompt_body(q_ref, k_ref, v_ref, o_ref, kb_ref, vt_ref, vt1_ref, km_ref, bias_ref, *, nb, group, hpb):
    i = pl.program_id(1)
    blk, dh = MB_BLOCK, MB_HEAD_DIM
    heads = range(hpb)

    @pl.when(i == 0)
    def _():
        for e in heads:
            for n in range(nb):
                kn = k_ref[n * blk:(n + 1) * blk, e * dh:(e + 1) * dh]
                u = n % group
                kb_ref[e, n // group, u * blk:(u + 1) * blk, :] = kn.astype(BF16)
                km_ref[e, n:n + 1, :] = jnp.mean(kn, axis=0, keepdims=True)
                vn_t = v_ref[n * blk:(n + 1) * blk, e * dh:(e + 1) * dh].T.astype(BF16)
                vt_ref[e, n // group, :, u * blk:(u + 1) * blk] = vn_t
                vt1_ref[e, n] = vn_t

    q_t = [q_ref[:, e * dh:(e + 1) * dh].T for e in heads]
    for e in heads:
        q_hi, q_lo = _split2(q_t[e])
        k_hi, k_lo = _split2(km_ref[e])
        gs = _dot(k_hi, q_hi) + _dot(k_hi, q_lo) + _dot(k_lo, q_hi)
        picked, _ = _top_blocks(gs, i)
        bias_ref[e] = jnp.where(picked > 0.0, 0.0, _NEG_INF)

    q_s = [(q_t[e] * (dh ** -0.5 * _LOG2E)).astype(BF16) for e in heads]
    own = pl.multiple_of((i % group) * blk, blk)
    kr = lax.broadcasted_iota(jnp.int32, (blk, blk), 0)
    qc = lax.broadcasted_iota(jnp.int32, (blk, blk), 1)
    s = [jnp.where(kr <= qc, _dot(kb_ref[e, i // group, pl.ds(own, blk), :], q_s[e]), _NEG_INF)
         for e in heads]
    m = [jnp.max(s[e], axis=0, keepdims=True) for e in heads]
    p = [jnp.exp2(s[e] - m[e]) for e in heads]
    l = [jnp.sum(p[e], axis=0, keepdims=True) for e in heads]
    acc = [_dot(vt1_ref[e, i], p[e].astype(BF16)) for e in heads]

    def step(jj, carry):
        m, l, acc = carry
        s = [_dot(kb_ref[e, jj], q_s[e]) for e in heads]
        parts = [[s[e][u * blk:(u + 1) * blk] + bias_ref[e, pl.ds(jj * group + u, 1), :] for u in range(group)]
                 for e in heads]
        m_new = []
        for e in heads:
            mx = m[e]
            for t_u in parts[e]:
                mx = jnp.maximum(mx, jnp.max(t_u, axis=0, keepdims=True))
            m_new.append(mx)
        alpha = [jnp.exp2(m[e] - m_new[e]) for e in heads]
        ps = [[jnp.exp2(t_u - m_new[e]) for t_u in parts[e]] for e in heads]
        l_new = []
        for e in heads:
            tot = alpha[e] * l[e]
            for p_u in ps[e]:
                tot = tot + jnp.sum(p_u, axis=0, keepdims=True)
            l_new.append(tot)
        pv = [_dot(vt_ref[e, jj], jnp.concatenate([p_u.astype(BF16) for p_u in ps[e]], axis=0)) for e in heads]
        acc_new = [alpha[e] * acc[e] + pv[e] for e in heads]
        return tuple(m_new), tuple(l_new), tuple(acc_new)

    m, l, acc = lax.fori_loop(0, (i + group - 1) // group, step, (tuple(m), tuple(l), tuple(acc)))
    for e in heads:
        o_ref[:, e * dh:(e + 1) * dh] = (acc[e] * (1.0 / l[e])).T.astype(o_ref.dtype)


def _moba_prompt(q, k, v, n_heads):
    t = q.shape[0]
    blk, dh = MB_BLOCK, MB_HEAD_DIM
    assert t % blk == 0
    nb = t // blk
    group = 4 if nb % 4 == 0 else 1
    hpb = 2 if n_heads % 2 == 0 else 1
    return pl.pallas_call(
        functools.partial(_moba_prompt_body, nb=nb, group=group, hpb=hpb),
        grid=(n_heads // hpb, nb),
        in_specs=[pl.BlockSpec((blk, hpb * dh), lambda h, i: (i, h)),
                  pl.BlockSpec((t, hpb * dh), lambda h, i: (0, h)),
                  pl.BlockSpec((t, hpb * dh), lambda h, i: (0, h))],
        out_specs=pl.BlockSpec((blk, hpb * dh), lambda h, i: (i, h)),
        out_shape=jax.ShapeDtypeStruct((t, n_heads * dh), BF16),
        scratch_shapes=[pltpu.VMEM((hpb, nb // group, group * blk, dh), BF16),
                        pltpu.VMEM((hpb, nb // group, dh, group * blk), BF16),
                        pltpu.VMEM((hpb, nb, dh, blk), BF16),
                        pltpu.VMEM((hpb, nb, dh), F32), pltpu.VMEM((hpb, nb, blk), F32)],
        compiler_params=_params(2),
        name="moba_prompt",
    )(q, k, v)


def _page_mean_body(pt_ref, *refs, pages):
    page_refs, o_ref = refs[:pages], refs[pages]
    n = pl.program_id(1)
    ppb = MB_BLOCK // PAGE_SIZE
    blocks = pages // ppb
    for u in range(blocks):
        tot = jnp.sum(page_refs[u * ppb][0, 0], axis=0)
        for e in range(1, ppb):
            tot = tot + jnp.sum(page_refs[u * ppb + e][0, 0], axis=0)
        o_ref[0, n * blocks + u] = tot * (1.0 / MB_BLOCK)


def _page_means(cache_k, page_table):
    b, n_pages = page_table.shape
    _, _, page, n_heads, dh = cache_k.shape
    ppb = MB_BLOCK // PAGE_SIZE
    assert page == PAGE_SIZE and n_pages % ppb == 0
    nblk = n_pages // ppb
    pages = 8 if n_pages % 8 == 0 else ppb

    def page_map(e):
        return lambda s, n, pt: (0, pt[s, n * pages + e], 0, 0, 0)

    grid_spec = pltpu.PrefetchScalarGridSpec(
        num_scalar_prefetch=1,
        grid=(b, n_pages // pages),
        in_specs=[pl.BlockSpec((1, 1, page, n_heads, dh), page_map(e)) for e in range(pages)],
        out_specs=pl.BlockSpec((1, nblk, n_heads, dh), lambda s, n, pt: (s, 0, 0, 0)),
    )
    return pl.pallas_call(
        functools.partial(_page_mean_body, pages=pages),
        grid_spec=grid_spec,
        out_shape=jax.ShapeDtypeStruct((b, nblk, n_heads, dh), F32),
        compiler_params=_params(2),
        name="moba_page_means",
    )(page_table, *([cache_k] * pages))


def _sample_gate_body(q_ref, km_ref, o_ref):
    km = km_ref[0]
    nblk, n_heads, dh = km.shape
    prod = (km * q_ref[0]).reshape(nblk * n_heads, dh)
    ones = jnp.ones((dh, dh), BF16)
    hi, mid, lo = _split3(prod)
    gs = (_dot(hi, ones) + _dot(mid, ones) + _dot(lo, ones)).reshape(nblk, n_heads, dh)
    _, order = _top_blocks(gs, nblk)
    order.append(jnp.zeros((8 - MB_TOPK, n_heads, dh), F32))
    o_ref[0] = jnp.concatenate(order, axis=0).astype(jnp.int32)


def _sample_gate(q_s3, kmean):
    b, nblk, n_heads, dh = kmean.shape
    return pl.pallas_call(
        _sample_gate_body,
        grid=(b,),
        in_specs=[pl.BlockSpec((1, n_heads, dh), lambda s: (s, 0, 0)),
                  pl.BlockSpec((1, nblk, n_heads, dh), lambda s: (s, 0, 0, 0))],
        out_specs=pl.BlockSpec((1, 8, n_heads, dh), lambda s: (s, 0, 0, 0)),
        out_shape=jax.ShapeDtypeStruct((b, 8, n_heads, dh), jnp.int32),
        compiler_params=_params(1),
        name="moba_sample_gate",
    )(q_s3, kmean)


def _sample_attn_body(pt_ref, top_ref, q_ref, kn_ref, vn_ref, ck_ref, cv_ref, o_ref, kbuf, vbuf, sem, *, n_heads):
    s = pl.program_id(0)
    ppb = MB_BLOCK // PAGE_SIZE
    n_pg = MB_TOPK * ppb
    dh = MB_HEAD_DIM

    def copies(seq, slot):
        out = []
        for h in range(n_heads):
            for t in range(MB_TOPK):
                blk_id = top_ref[seq, h * MB_TOPK + t]
                for e in range(ppb):
                    page = pt_ref[seq, ppb * blk_id + e]
                    out.append(pltpu.make_async_copy(ck_ref.at[0, page, :, h, :], kbuf.at[slot, h, t * ppb + e],
                                                     sem.at[0, slot]))
                    out.append(pltpu.make_async_copy(cv_ref.at[0, page, :, h, :], vbuf.at[slot, h, t * ppb + e],
                                                     sem.at[1, slot]))
        return out

    slot = s % 2

    @pl.when(s == 0)
    def _():
        for cp in copies(0, 0):
            cp.start()

    @pl.when(s + 1 < pl.num_programs(0))
    def _():
        for cp in copies(s + 1, 1 - slot):
            cp.start()

    for cp in copies(s, slot):
        cp.wait()

    scale = dh ** -0.5
    for h in range(n_heads):
        q = q_ref[0, h:h + 1, :] * scale
        keys = kbuf[slot, h].reshape(n_pg * PAGE_SIZE, dh).astype(BF16)
        vals = vbuf[slot, h].reshape(n_pg * PAGE_SIZE, dh).astype(BF16)
        sc = _dot_nt(jnp.broadcast_to(q, (8, dh)).astype(BF16), keys)[0:1, :]
        s_self = jnp.sum(q * kn_ref[0, h:h + 1, :], axis=-1, keepdims=True)
        m = jnp.maximum(jnp.max(sc, axis=-1, keepdims=True), s_self)
        p = jnp.exp(sc - m)
        p_self = jnp.exp(s_self - m)
        l = jnp.sum(p, axis=-1, keepdims=True) + p_self
        pv = _dot(jnp.broadcast_to(p, (8, p.shape[1])).astype(BF16), vals)[0:1, :]
        o_ref[0, h:h + 1, :] = (pv + vn_ref[0, h:h + 1, :] * p_self) * (1.0 / l)


def _sample_attn(q_s3, k_s3, v_s3, cache_k, cache_v, page_table, top, n_heads):
    b = q_s3.shape[0]
    dh = MB_HEAD_DIM
    n_pg = MB_TOPK * (MB_BLOCK // PAGE_SIZE)
    row = pl.BlockSpec((1, n_heads, dh), lambda s, pt, tp: (s, 0, 0))
    hbm = pl.BlockSpec(memory_space=pl.ANY)
    grid_spec = pltpu.PrefetchScalarGridSpec(
        num_scalar_prefetch=2,
        grid=(b,),
        in_specs=[row, row, row, hbm, hbm],
        out_specs=row,
        scratch_shapes=[pltpu.VMEM((2, n_heads, n_pg, PAGE_SIZE, dh), F32),
                        pltpu.VMEM((2, n_heads, n_pg, PAGE_SIZE, dh), F32),
                        pltpu.SemaphoreType.DMA((2, 2))],
    )
    return pl.pallas_call(
        functools.partial(_sample_attn_body, n_heads=n_heads),
        grid_spec=grid_spec,
        out_shape=jax.ShapeDtypeStruct((b, n_heads, dh), F32),
        compiler_params=_params(1),
        name="moba_sample_attn",
    )(page_table, top, q_s3, k_s3, v_s3, cache_k, cache_v)


def _layer(xp, xs, cache_k, cache_v, wkv_s, shift_s, page_table, norm_mix, w_in, rwkv_mu, w0, w2, a0, a2, g2,
           k_k, k_a, r_k, lnx_g, lnx_b, up_rw, up_mb, w_o, norm_ffn, w_gate, w_up, w_down):
    t, d_model = xp.shape
    b = xs.shape[0]
    n_rw_heads = r_k.shape[0]
    rw_width = n_rw_heads * RW_HEAD_DIM
    n_mb_heads = cache_k.shape[3]
    mb_width = n_mb_heads * MB_HEAD_DIM
    rw_proj = rwkv_mu.shape[0]
    lora = DECAY_LORA + ICLR_LORA + GATE_LORA
    assert rw_proj == 3 * rw_width + lora
    rw_pad = 3 * rw_width + LORA_TAIL
    rw_col0 = 3 * mb_width
    gate_col0 = rw_col0 + rw_proj
    d_ff = w_gate.shape[1]

    xn_p = _rmsnorm(xp, norm_mix, BF16)
    xn_s = _rmsnorm(xs, norm_mix, BF16)

    def proj(name, col0, n_out, tn, ep=_ep_identity, w=w_in, dt=F32):
        (o_p,), (o_s,) = _fused_matmul(name, [xn_p], [xn_s], [w], [col0], n_out, _pick_tile(n_out, tn), 1024,
                                       [], ep, [dt])
        return o_p, o_s

    q_p, q_s = proj("proj_q", 0, mb_width, 512)
    k_p, k_s = proj("proj_k", mb_width, mb_width, 512)
    v_p, v_s = proj("proj_v", 2 * mb_width, mb_width, 512)
    p_rw_p, p_rw_s = proj("proj_rw", rw_col0, rw_pad, 512)
    w_gates = _realign_cols(w_in, gate_col0)
    sg_p, sg_s = proj("proj_gates", 0, 2 * d_model, 512, ep=_ep_sigmoid, w=w_gates, dt=BF16)

    pad_row = lambda a: jnp.pad(a, (0, rw_pad - rw_proj)).reshape(1, rw_pad)
    lora_pad = lambda a, r0: jnp.pad(a, ((r0, LORA_TAIL - r0 - a.shape[0]), (0, 0))).astype(BF16)
    prm = dict(
        mu=pad_row(rwkv_mu), w0=w0.reshape(1, rw_width), a0=a0.reshape(1, rw_width),
        k_k=k_k.reshape(1, rw_width), k_a=k_a.reshape(1, rw_width), r_k=r_k.reshape(1, rw_width),
        lnx_g=lnx_g.reshape(1, rw_width), lnx_b=lnx_b.reshape(1, rw_width),
        w2p=lora_pad(w2, 0), a2p=lora_pad(a2, DECAY_LORA), g2p=lora_pad(g2, DECAY_LORA + ICLR_LORA))
    y_rw_p, st_pairs = _rw_scan_prompt(p_rw_p, prm, rw_width)
    n_pair = rw_width // LANES
    st5 = st_pairs.reshape(n_pair, 2, RW_HEAD_DIM, 2, RW_HEAD_DIM)
    wkv_p = jnp.stack([st5[:, 0, :, 0, :], st5[:, 1, :, 1, :]], axis=1)
    wkv_p = jnp.swapaxes(wkv_p, -1, -2).reshape(n_rw_heads, RW_HEAD_DIM, RW_HEAD_DIM)
    shift_p = p_rw_p[t - 1, :rw_proj]

    prev_s = jnp.pad(shift_s, ((0, 0), (0, rw_pad - rw_proj)))
    rows = _rw_sample_prep(p_rw_s, prev_s, prm, rw_width)
    rows = [a.reshape(b, n_rw_heads, 1, RW_HEAD_DIM) for a in rows]
    head_prm = [a.reshape(n_rw_heads, 1, RW_HEAD_DIM) for a in (k_k, k_a, r_k, lnx_g, lnx_b)]
    wkv_s_new, y_rw_s = _rw_sample_step(wkv_s, rows, head_prm)
    y_rw_s = y_rw_s.reshape(b, rw_width).astype(BF16)
    shift_s_new = p_rw_s[:, :rw_proj]

    y_mb_p = _moba_prompt(q_p, k_p, v_p, n_mb_heads)
    kmean_s = _page_means(cache_k, page_table)
    q_s3 = q_s.reshape(b, n_mb_heads, MB_HEAD_DIM)
    top = _sample_gate(q_s3, kmean_s)
    top = jnp.swapaxes(top[:, :MB_TOPK, :, 0], 1, 2).reshape(b, n_mb_heads * MB_TOPK)
    y_mb_s = _sample_attn(q_s3, k_s.reshape(b, n_mb_heads, MB_HEAD_DIM), v_s.reshape(b, n_mb_heads, MB_HEAD_DIM),
                          cache_k, cache_v, page_table, top, n_mb_heads)
    y_mb_s = y_mb_s.reshape(b, mb_width).astype(BF16)

    (mg_p,), (mg_s,) = _fused_matmul("merge", [y_rw_p, y_mb_p], [y_rw_s, y_mb_s], [up_rw, up_mb], [0, 0],
                                     d_model, _pick_tile(d_model, 512), 1024,
                                     [(sg_p, sg_s, 0), (sg_p, sg_s, d_model)], _ep_merge, [BF16])
    (x2_p,), (x2_s,) = _fused_matmul("out_proj", [mg_p], [mg_s], [w_o], [0], d_model, _pick_tile(d_model, 512), 1024,
                                     [(xp, xs, 0)], _ep_residual, [F32])
    xn2_p = _rmsnorm(x2_p, norm_ffn, BF16)
    xn2_s = _rmsnorm(x2_s, norm_ffn, BF16)
    (h_p,), (h_s,) = _fused_matmul("ffn_up", [xn2_p], [xn2_s], [w_gate, w_up], [0, 0],
                                   d_ff, _pick_tile(d_ff, 256), 1024, [], _ep_swiglu, [BF16], x_of=[0, 0])
    (x3_p,), (x3_s,) = _fused_matmul("ffn_down", [h_p], [h_s], [w_down.astype(BF16)], [0], d_model,
                                     _pick_tile(d_model, 512), 512, [(x2_p, x2_s, 0)], _ep_residual, [F32])
    return (x3_p, x3_s, k_p, v_p, wkv_p, shift_p, k_s, v_s, wkv_s_new, shift_s_new)


def kernel(x_prompt, x_sample, cache_k, cache_v, state_wkv, state_shift, page_table, norm_mix, w_in, rwkv_mu, w0, w2, a0, a2, g2, k_k, k_a, r_k, lnx_g, lnx_b, up_rw, up_mb, w_o, norm_ffn, w_gate, w_up, w_down, norm_final):
    depth = w_in.shape[0]
    assert depth == 1 and x_prompt.shape[0] == 1 and x_sample.shape[1] == 1
    t, d_model = x_prompt.shape[1], x_prompt.shape[2]
    b = x_sample.shape[0]
    n_mb_heads = cache_k.shape[3]
    n_rw_heads = r_k.shape[1]
    out = _layer(x_prompt[0], x_sample[:, 0], cache_k, cache_v, state_wkv[0], state_shift[0], page_table,
                 norm_mix[0], w_in[0], rwkv_mu[0], w0[0], w2[0], a0[0], a2[0], g2[0], k_k[0], k_a[0], r_k[0],
                 lnx_g[0], lnx_b[0], up_rw[0], up_mb[0], w_o[0], norm_ffn[0], w_gate[0], w_up[0], w_down[0])
    x3_p, x3_s, k_p, v_p, wkv_p, shift_p, k_s, v_s, wkv_s, shift_s = out
    y_p = _rmsnorm(x3_p, norm_final, F32)
    y_s = _rmsnorm(x3_s, norm_final, F32)
    rw_proj = rwkv_mu.shape[1]
    return (y_p.reshape(1, t, d_model), y_s.reshape(b, 1, d_model),
            k_p.reshape(1, 1, t, n_mb_heads, MB_HEAD_DIM), v_p.reshape(1, 1, t, n_mb_heads, MB_HEAD_DIM),
            wkv_p.reshape(1, 1, n_rw_heads, RW_HEAD_DIM, RW_HEAD_DIM), shift_p.reshape(1, 1, rw_proj),
            k_s.reshape(1, b, 1, n_mb_heads, MB_HEAD_DIM), v_s.reshape(1, b, 1, n_mb_heads, MB_HEAD_DIM),
            wkv_s.reshape(1, b, n_rw_heads, RW_HEAD_DIM, RW_HEAD_DIM), shift_s.reshape(1, b, rw_proj))
```

```python
import functools

import jax
import jax.numpy as jnp
from jax import lax
from jax.experimental import pallas as pl
from jax.experimental.pallas import tpu as pltpu

F32 = jnp.float32
BF16 = jnp.bfloat16

RMS_EPS = 1e-6
LNX_EPS = 64e-5
MB_BLOCK = 256
MB_TOPK = 3
MB_HEAD_DIM = 128
PAGE_SIZE = 128
RW_HEAD_DIM = 64
DECAY_LORA = 96
ICLR_LORA = 96
GATE_LORA = 256
LORA_TAIL = 512
SCAN_CHUNK = 64
LANES = 128
VMEM_LIMIT = 56 * 1024 * 1024

_NEG_INF = float("-inf")
_LOG2E = 1.4426950408889634


def _params(n_axes):
    return pltpu.CompilerParams(dimension_semantics=("arbitrary",) * n_axes,
                                vmem_limit_bytes=VMEM_LIMIT)


def _dot(a, b):
    return jnp.dot(a, b, preferred_element_type=F32)


def _dot_nt(a, b):
    return lax.dot_general(a, b, (((1,), (1,)), ((), ())), preferred_element_type=F32)


def _dot_tn(a, b):
    return lax.dot_general(a, b, (((0,), (0,)), ((), ())), preferred_element_type=F32)


def _split2(x):
    hi = x.astype(BF16)
    lo = (x - hi.astype(F32)).astype(BF16)
    return hi, lo


def _split3(x):
    hi = x.astype(BF16)
    r1 = x - hi.astype(F32)
    mid = r1.astype(BF16)
    lo = (r1 - mid.astype(F32)).astype(BF16)
    return hi, mid, lo


def _sigmoid(x):
    return 1.0 / (1.0 + jnp.exp(-x))


def _softplus(x):
    return jnp.maximum(x, 0.0) + jnp.log1p(jnp.exp(-jnp.abs(x)))


def _rmsnorm_body(x_ref, g_ref, o_ref):
    x = x_ref[...]
    ms = jnp.mean(x * x, axis=-1, keepdims=True)
    o_ref[...] = (x * lax.rsqrt(ms + RMS_EPS) * g_ref[...]).astype(o_ref.dtype)


def _rmsnorm(x, g, out_dtype, tm=256):
    m, d = x.shape
    tm = min(tm, m)
    assert m % tm == 0
    return pl.pallas_call(
        _rmsnorm_body,
        grid=(m // tm,),
        in_specs=[pl.BlockSpec((tm, d), lambda i: (i, 0)), pl.BlockSpec((1, d), lambda i: (0, 0))],
        out_specs=pl.BlockSpec((tm, d), lambda i: (i, 0)),
        out_shape=jax.ShapeDtypeStruct((m, d), out_dtype),
        compiler_params=_params(1),
        name="rmsnorm",
    )(x, g.reshape(1, d))


def _pick_tile(n, pref):
    t = pref
    while n % t:
        t //= 2
    assert t >= LANES
    return t


def _fused_matmul(name, xp, xs, ws, w_col0, n_out, tn, tm, extras, epilogue, out_dtypes, x_of=None, w_t=False):
    nd, ne, no, nx = len(ws), len(extras), len(out_dtypes), len(xp)
    mm = _dot_nt if w_t else _dot
    x_of = list(range(nd)) if x_of is None else x_of
    mp, ms = xp[0].shape[0], xs[0].shape[0]
    tm = min(tm, mp)
    assert mp % tm == 0 and n_out % tn == 0
    nj, ni = n_out // tn, mp // tm
    cast = [w.dtype != BF16 for w in ws]
    n_cast = sum(cast)
    cast_slot = [sum(cast[:k]) for k in range(nd)]

    def body(*refs):
        xp_r = refs[:nx]
        xs_r = refs[nx:2 * nx]
        w_r = refs[2 * nx:2 * nx + nd]
        base = 2 * nx + nd
        ep_r = refs[base:base + ne]
        es_r = refs[base + ne:base + 2 * ne]
        base += 2 * ne
        op_r = refs[base:base + no]
        os_r = refs[base + no:base + 2 * no]
        wb_r = refs[base + 2 * no:]
        i = pl.program_id(1)

        def weight(k):
            return wb_r[cast_slot[k]][...] if cast[k] else w_r[k][...]

        @pl.when(i == 0)
        def _():
            for k in range(nd):
                if cast[k]:
                    wb_r[cast_slot[k]][...] = w_r[k][...].astype(BF16)
            accs = [mm(xs_r[x_of[k]][...], weight(k)) for k in range(nd)]
            outs = epilogue(accs, [e[...] for e in es_r])
            for o, val in zip(os_r, outs):
                o[...] = val.astype(o.dtype)

        accs = [mm(xp_r[x_of[k]][...], weight(k)) for k in range(nd)]
        outs = epilogue(accs, [e[...] for e in ep_r])
        for o, val in zip(op_r, outs):
            o[...] = val.astype(o.dtype)

    in_specs = []
    for k in range(nx):
        kk = xp[k].shape[1]
        in_specs.append(pl.BlockSpec((tm, kk), lambda j, i: (i, 0)))
    for k in range(nx):
        kk = xs[k].shape[1]
        in_specs.append(pl.BlockSpec((ms, kk), lambda j, i: (0, 0)))
    for k in range(nd):
        c0 = w_col0[k]
        if not w_t:
            assert c0 % tn == 0
            in_specs.append(pl.BlockSpec((ws[k].shape[0], tn), functools.partial(lambda j, i, c: (0, j + c), c=c0 // tn)))
        elif c0 % tn == 0:
            in_specs.append(pl.BlockSpec((tn, ws[k].shape[1]), functools.partial(lambda j, i, c: (j + c, 0), c=c0 // tn)))
        else:
            assert c0 % 8 == 0
            in_specs.append(pl.BlockSpec((pl.Element(tn), pl.Element(ws[k].shape[1])),
                                         functools.partial(lambda j, i, c: (pl.multiple_of(c + j * tn, 8), 0), c=c0)))
    for (_, _, c0) in extras:
        assert c0 % tn == 0
        in_specs.append(pl.BlockSpec((tm, tn), functools.partial(lambda j, i, c: (i, j + c), c=c0 // tn)))
    for (_, _, c0) in extras:
        in_specs.append(pl.BlockSpec((ms, tn), functools.partial(lambda j, i, c: (0, j + c), c=c0 // tn)))
    out_specs = [pl.BlockSpec((tm, tn), lambda j, i: (i, j)) for _ in range(no)]
    out_specs += [pl.BlockSpec((ms, tn), lambda j, i: (0, j)) for _ in range(no)]
    out_shape = [jax.ShapeDtypeStruct((mp, n_out), dt) for dt in out_dtypes]
    out_shape += [jax.ShapeDtypeStruct((ms, n_out), dt) for dt in out_dtypes]
    scratch = [pltpu.VMEM((tn, ws[k].shape[1]) if w_t else (ws[k].shape[0], tn), BF16) for k in range(nd) if cast[k]]
    assert len(scratch) == n_cast
    args = list(xp) + list(xs) + list(ws) + [e[0] for e in extras] + [e[1] for e in extras]
    outs = pl.pallas_call(
        body,
        grid=(nj, ni),
        in_specs=in_specs,
        out_specs=out_specs,
        out_shape=out_shape,
        scratch_shapes=scratch,
        compiler_params=_params(2),
        name=name,
    )(*args)
    return outs[:no], outs[no:]


def _ep_identity(accs, extras):
    return (accs[0],)


def _ep_sigmoid(accs, extras):
    return (_sigmoid(accs[0]),)


def _ep_merge(accs, extras):
    return (extras[0].astype(F32) * accs[0] + extras[1].astype(F32) * accs[1],)


def _ep_residual(accs, extras):
    return (extras[0] + accs[0],)


def _ep_swiglu(accs, extras):
    g = accs[0]
    return (g * _sigmoid(g) * accs[1],)


def _rw_prep(p, prev, mu, w0, a0, w2p, a2p, g2p, width):
    pm = p + (prev - p) * mu
    r = pm[:, :width]
    k = pm[:, width:2 * width]
    v = pm[:, 2 * width:3 * width]
    tail = pm[:, 3 * width:3 * width + LORA_TAIL]
    lw = w0 + _dot(jnp.tanh(tail).astype(BF16), w2p)
    w_log = -_softplus(-lw) - 0.5
    logd = -jnp.exp(w_log)
    a_lr = _sigmoid(a0 + _dot(tail.astype(BF16), a2p))
    g = _dot(_sigmoid(tail).astype(BF16), g2p)
    return r, k, v, logd, a_lr, g


def _seg_ones():
    ri = lax.broadcasted_iota(jnp.int32, (LANES, LANES), 0) // RW_HEAD_DIM
    ci = lax.broadcasted_iota(jnp.int32, (LANES, LANES), 1) // RW_HEAD_DIM
    return jnp.where(ri == ci, 1.0, 0.0).astype(BF16)


def _seg_sum(x, ones_bd):
    return _dot(x.astype(BF16), ones_bd)


def _rw_scan_body(p_ref, mu_ref, w0_ref, a0_ref, kk_ref, ka_ref, rk_ref, lg_ref, lb_ref,
                  w2_ref, a2_ref, g2_ref, y_ref, st_ref, pbuf_ref, state_ref, *, width):
    c = pl.program_id(0)
    ch = SCAN_CHUNK
    n_pair = width // LANES

    @pl.when(c == 0)
    def _():
        pbuf_ref[...] = jnp.zeros_like(pbuf_ref)
        state_ref[...] = jnp.zeros_like(state_ref)

    pbuf_ref[pl.ds(8, ch), :] = p_ref[...]
    p = pbuf_ref[pl.ds(8, ch), :]
    prev = pbuf_ref[pl.ds(7, ch), :]
    r, k, v, logd, a_lr, g = _rw_prep(p, prev, mu_ref[...], w0_ref[...], a0_ref[...],
                                      w2_ref[...], a2_ref[...], g2_ref[...], width)
    pbuf_ref[pl.ds(0, 8), :] = pbuf_ref[pl.ds(ch, 8), :]

    ti = lax.broadcasted_iota(jnp.int32, (ch, ch), 0)
    tj = lax.broadcasted_iota(jnp.int32, (ch, ch), 1)
    ltri = jnp.where(ti >= tj, 1.0, 0.0).astype(BF16)
    d_hi, d_mid, d_lo = _split3(logd)
    lc = _dot(ltri, d_hi) + _dot(ltri, d_mid) + _dot(ltri, d_lo)
    lc_end = lc[ch - 1:ch, :]
    w_t = jnp.exp(lc)
    w_tm1 = jnp.exp(lc - logd)
    w_inv = jnp.exp(-lc)
    w_rem = jnp.exp(lc_end - lc)
    w_end = jnp.exp(lc_end)

    ones_bd = _seg_ones()
    kkr = k * kk_ref[...]
    kf = k * (1.0 + (a_lr - 1.0) * ka_ref[...])
    rkk = r * kf * rk_ref[...]

    lane = lax.broadcasted_iota(jnp.int32, (ch, LANES), 1)
    m0 = jnp.where(lane < RW_HEAD_DIM, 1.0, 0.0)
    m1 = 1.0 - m0

    def stack(z):
        return jnp.concatenate([z * m0, z * m1], axis=0)

    ri = lax.broadcasted_iota(jnp.int32, (2 * ch, 2 * ch), 0) % ch
    ci = lax.broadcasted_iota(jnp.int32, (2 * ch, 2 * ch), 1) % ch
    tri_strict = jnp.where(ri > ci, 1.0, 0.0)
    tri_incl = jnp.where(ri >= ci, 1.0, 0.0)
    er = lax.broadcasted_iota(jnp.int32, (LANES, LANES), 0)
    ec = lax.broadcasted_iota(jnp.int32, (LANES, LANES), 1)
    eye = jnp.where(er == ec, 1.0, 0.0)

    pairs = range(n_pair)
    sls = [slice(gp * LANES, (gp + 1) * LANES) for gp in pairs]
    h2 = 2 * ch

    def rows_of(z_list):
        return jnp.concatenate(z_list, axis=0)

    kkr_g = [kkr[:, sl] for sl in sls]
    ss = _seg_sum(rows_of([z * z for z in kkr_g]), ones_bd)
    kk_g = [kkr_g[gp] / jnp.maximum(jnp.sqrt(ss[gp * ch:(gp + 1) * ch]), 1e-12) for gp in pairs]
    braw_g = [kk_g[gp] * a_lr[:, sls[gp]] for gp in pairs]
    kf_g = [kf[:, sl] for sl in sls]
    a_s = [stack(-kk_g[gp] * w_tm1[:, sls[gp]]).astype(BF16) for gp in pairs]
    r_s = [stack(r[:, sls[gp]] * w_t[:, sls[gp]]).astype(BF16) for gp in pairs]
    v_s = [stack(v[:, sl]).astype(BF16) for sl in sls]
    rhs = [jnp.concatenate([stack(braw_g[gp] * w_inv[:, sls[gp]]), stack(kf_g[gp] * w_inv[:, sls[gp]])],
                           axis=0).astype(BF16) for gp in pairs]
    upd_l = [jnp.concatenate([stack(braw_g[gp] * w_rem[:, sls[gp]]), stack(kf_g[gp] * w_rem[:, sls[gp]])],
                             axis=0).astype(BF16) for gp in pairs]
    gram = [_dot_nt(jnp.concatenate([a_s[gp], r_s[gp]], axis=0), rhs[gp]) for gp in pairs]
    pw = [(gram[gp][:h2, :h2] * tri_strict).astype(BF16) for gp in pairs]
    n_bd = [(gram[gp][:h2, h2:] * tri_strict).astype(BF16) for gp in pairs]
    pb = [(gram[gp][h2:, :h2] * tri_incl).astype(BF16) for gp in pairs]
    pk = [(gram[gp][h2:, h2:] * tri_incl).astype(BF16) for gp in pairs]

    st = [state_ref[gp] for gp in pairs]
    st_b = [z.astype(BF16) for z in st]
    sa = [_dot(jnp.concatenate([a_s[gp], n_bd[gp]], axis=1), jnp.concatenate([st_b[gp], v_s[gp]], axis=0))
          for gp in pairs]
    n_steps = ch.bit_length() - 1
    for s in range(n_steps):
        sa = [sa[gp] + _dot(pw[gp], sa[gp].astype(BF16)) for gp in pairs]
        if s + 1 < n_steps:
            pw = [_dot(pw[gp], pw[gp]).astype(BF16) for gp in pairs]
    sa_b = [z.astype(BF16) for z in sa]
    y_s = [_dot(jnp.concatenate([r_s[gp], pb[gp], pk[gp]], axis=1),
                jnp.concatenate([st_b[gp], sa_b[gp], v_s[gp]], axis=0)) for gp in pairs]
    upd = [_dot_tn(upd_l[gp], jnp.concatenate([sa_b[gp], v_s[gp]], axis=0)) for gp in pairs]
    for gp in pairs:
        w_col = jnp.sum(eye * w_end[:, sls[gp]], axis=1, keepdims=True)
        state_ref[gp] = st[gp] * w_col + upd[gp]

    y = rows_of([y_s[gp][:ch] + y_s[gp][ch:] for gp in pairs])
    mean = _seg_sum(y, ones_bd) * (1.0 / RW_HEAD_DIM)
    d = y - mean
    var = _seg_sum(d * d, ones_bd) * (1.0 / RW_HEAD_DIM)
    yn = d * lax.rsqrt(var + LNX_EPS)
    bonus = _seg_sum(rows_of([rkk[:, sl] for sl in sls]), ones_bd)
    for gp in pairs:
        sl = sls[gp]
        rs = slice(gp * ch, (gp + 1) * ch)
        out = (yn[rs] * lg_ref[:, sl] + lb_ref[:, sl] + bonus[rs] * v[:, sl]) * g[:, sl]
        y_ref[:, sl] = out.astype(y_ref.dtype)

    @pl.when(c == pl.num_programs(0) - 1)
    def _():
        st_ref[...] = state_ref[...]


def _rw_scan_prompt(p_rw, prm, width):
    t, pw = p_rw.shape
    ch = SCAN_CHUNK
    assert t % ch == 0 and width % LANES == 0
    n_pair = width // LANES
    row = lambda n: pl.BlockSpec((1, n), lambda c: (0, 0))
    full = lambda a: pl.BlockSpec(a.shape, lambda c: (0, 0))
    return pl.pallas_call(
        functools.partial(_rw_scan_body, width=width),
        grid=(t // ch,),
        in_specs=[pl.BlockSpec((ch, pw), lambda c: (c, 0)), row(pw)] + [row(width)] * 7
                 + [full(prm["w2p"]), full(prm["a2p"]), full(prm["g2p"])],
        out_specs=[pl.BlockSpec((ch, width), lambda c: (c, 0)),
                   pl.BlockSpec((n_pair, LANES, LANES), lambda c: (0, 0, 0))],
        out_shape=[jax.ShapeDtypeStruct((t, width), BF16),
                   jax.ShapeDtypeStruct((n_pair, LANES, LANES), F32)],
        scratch_shapes=[pltpu.VMEM((ch + 8, pw), F32), pltpu.VMEM((n_pair, LANES, LANES), F32)],
        compiler_params=_params(1),
        name="rwkv_scan",
    )(p_rw, prm["mu"], prm["w0"], prm["a0"], prm["k_k"], prm["k_a"], prm["r_k"], prm["lnx_g"], prm["lnx_b"],
      prm["w2p"], prm["a2p"], prm["g2p"])


def _rw_sample_prep_body(p_ref, prev_ref, mu_ref, w0_ref, a0_ref, w2_ref, a2_ref, g2_ref,
                         r_ref, k_ref, v_ref, d_ref, alr_ref, g_ref, *, width):
    r, k, v, logd, a_lr, g = _rw_prep(p_ref[...], prev_ref[...], mu_ref[...], w0_ref[...], a0_ref[...],
                                      w2_ref[...], a2_ref[...], g2_ref[...], width)
    r_ref[...] = r
    k_ref[...] = k
    v_ref[...] = v
    d_ref[...] = logd
    alr_ref[...] = a_lr
    g_ref[...] = g


def _rw_sample_prep(p_s, prev_s, prm, width):
    b = p_s.shape[0]
    args = (p_s, prev_s, prm["mu"], prm["w0"], prm["a0"], prm["w2p"], prm["a2p"], prm["g2p"])
    return pl.pallas_call(
        functools.partial(_rw_sample_prep_body, width=width),
        grid=(1,),
        in_specs=[pl.BlockSpec(a.shape, lambda i: (0, 0)) for a in args],
        out_specs=[pl.BlockSpec((b, width), lambda i: (0, 0))] * 6,
        out_shape=[jax.ShapeDtypeStruct((b, width), F32)] * 6,
        compiler_params=_params(1),
        name="rwkv_sample_prep",
    )(*args)


def _rw_sample_step_body(s_ref, r_ref, k_ref, v_ref, d_ref, alr_ref, g_ref,
                         kk_ref, ka_ref, rk_ref, lg_ref, lb_ref, so_ref, y_ref):
    s = s_ref[0]
    r, k, v = r_ref[0], k_ref[0], v_ref[0]
    a_lr, g = alr_ref[0], g_ref[0]
    w = jnp.exp(d_ref[0])
    kkr = k * kk_ref[...]
    kk = kkr / jnp.maximum(jnp.sqrt(jnp.sum(kkr * kkr, axis=-1, keepdims=True)), 1e-12)
    kf = k * (1.0 + (a_lr - 1.0) * ka_ref[...])
    a = -kk
    b = kk * a_lr
    n = RW_HEAD_DIM
    eye = (lax.broadcasted_iota(jnp.int32, (n, n), 0) == lax.broadcasted_iota(jnp.int32, (n, n), 1))
    sa = jnp.sum(s * a, axis=-1, keepdims=True)
    v_col = jnp.sum(jnp.where(eye, v, 0.0), axis=-1, keepdims=True)
    s1 = s * w + sa * b + v_col * kf
    so_ref[0] = s1
    y_col = jnp.sum(s1 * r, axis=-1, keepdims=True)
    y = jnp.sum(jnp.where(eye, y_col, 0.0), axis=-2, keepdims=True)
    mean = jnp.mean(y, axis=-1, keepdims=True)
    d = y - mean
    var = jnp.mean(d * d, axis=-1, keepdims=True)
    yn = d * lax.rsqrt(var + LNX_EPS) * lg_ref[...] + lb_ref[...]
    bonus = jnp.sum(r * kf * rk_ref[...], axis=-1, keepdims=True) * v
    y_ref[0] = (yn + bonus) * g


def _rw_sample_step(state, rows, head_prm):
    b, h, n, _ = state.shape
    vec = pl.BlockSpec((1, h, 1, n), lambda i: (i, 0, 0, 0))
    par = pl.BlockSpec((h, 1, n), lambda i: (0, 0, 0))
    mat = pl.BlockSpec((1, h, n, n), lambda i: (i, 0, 0, 0))
    return pl.pallas_call(
        _rw_sample_step_body,
        grid=(b,),
        in_specs=[mat] + [vec] * 6 + [par] * 5,
        out_specs=[mat, vec],
        out_shape=[jax.ShapeDtypeStruct((b, h, n, n), F32), jax.ShapeDtypeStruct((b, h, 1, n), F32)],
        compiler_params=_params(1),
        name="rwkv_sample_step",
    )(state, *rows, *head_prm)


def _top_blocks(gs, n_valid):
    nb = gs.shape[0]
    rowi = lax.broadcasted_iota(jnp.int32, gs.shape, 0)
    rowf = rowi.astype(F32)
    gm = jnp.where(rowi < n_valid, gs, _NEG_INF)
    picked = jnp.zeros(gs.shape, F32)
    order = []
    for _ in range(MB_TOPK):
        mx = jnp.max(gm, axis=0, keepdims=True)
        first = jnp.min(jnp.where(gm == mx, rowf, float(nb)), axis=0, keepdims=True)
        hit = jnp.where(rowf == first, jnp.where(mx > _NEG_INF, 1.0, 0.0), 0.0)
        picked = picked + hit
        gm = jnp.where(hit > 0.0, _NEG_INF, gm)
        order.append(first)
    return picked, order


def _moba_prompt_body(pt_ref, q_ref, k_ref, v_ref, *refs, nb, group, hpb, pages):
    page_refs = refs[:pages]
    o_ref, pm_ref, kb_ref, vt_ref, vt1_ref, km_ref, bias_ref = refs[pages:]
    i = pl.program_id(1)
    blk, dh = MB_BLOCK, MB_HEAD_DIM
    heads = range(hpb)

    if pages:
        ppb = MB_BLOCK // PAGE_SIZE
        step = pl.program_id(0) * pl.num_programs(1) + i
        first = (step % (pm_ref.shape[1] * ppb // pages)) * (pages // ppb)
        for u in range(pages // ppb):
            tot = jnp.sum(page_refs[u * ppb][0, 0], axis=0)
            for e in range(1, ppb):
                tot = tot + jnp.sum(page_refs[u * ppb + e][0, 0], axis=0)
            pm_ref[0, first + u] = tot * (1.0 / MB_BLOCK)

    @pl.when(i == 0)
    def _():
        for e in heads:
            for n in range(nb):
                kn = k_ref[n * blk:(n + 1) * blk, e * dh:(e + 1) * dh]
                u = n % group
                kb_ref[e, n // group, u * blk:(u + 1) * blk, :] = kn.astype(BF16)
                km_ref[e, n:n + 1, :] = jnp.mean(kn, axis=0, keepdims=True)
                vn_t = v_ref[n * blk:(n + 1) * blk, e * dh:(e + 1) * dh].T.astype(BF16)
                vt_ref[e, n // group, :, u * blk:(u + 1) * blk] = vn_t
                vt1_ref[e, n] = vn_t

    q_t = [q_ref[:, e * dh:(e + 1) * dh].T for e in heads]
    for e in heads:
        q_hi, q_lo = _split2(q_t[e])
        k_hi, k_lo = _split2(km_ref[e])
        gs = _dot(k_hi, q_hi) + _dot(k_hi, q_lo) + _dot(k_lo, q_hi)
        picked, _ = _top_blocks(gs, i)
        bias_ref[e] = jnp.where(picked > 0.0, 0.0, _NEG_INF)

    q_s = [(q_t[e] * (dh ** -0.5 * _LOG2E)).astype(BF16) for e in heads]
    own = pl.multiple_of((i % group) * blk, blk)
    kr = lax.broadcasted_iota(jnp.int32, (blk, blk), 0)
    qc = lax.broadcasted_iota(jnp.int32, (blk, blk), 1)
    s = [jnp.where(kr <= qc, _dot(kb_ref[e, i // group, pl.ds(own, blk), :], q_s[e]), _NEG_INF)
         for e in heads]
    m = [jnp.max(s[e], axis=0, keepdims=True) for e in heads]
    p = [jnp.exp2(s[e] - m[e]) for e in heads]
    l = [jnp.sum(p[e], axis=0, keepdims=True) for e in heads]
    acc = [_dot(vt1_ref[e, i], p[e].astype(BF16)) for e in heads]

    def step(jj, carry):
        m, l, acc = carry
        s = [_dot(kb_ref[e, jj], q_s[e]) for e in heads]
        parts = [[s[e][u * blk:(u + 1) * blk] + bias_ref[e, pl.ds(jj * group + u, 1), :] for u in range(group)]
                 for e in heads]
        m_new = []
        for e in heads:
            mx = m[e]
            for t_u in parts[e]:
                mx = jnp.maximum(mx, jnp.max(t_u, axis=0, keepdims=True))
            m_new.append(mx)
        alpha = [jnp.exp2(m[e] - m_new[e]) for e in heads]
        ps = [[jnp.exp2(t_u - m_new[e]) for t_u in parts[e]] for e in heads]
        l_new = []
        for e in heads:
            tot = alpha[e] * l[e]
            for p_u in ps[e]:
                tot = tot + jnp.sum(p_u, axis=0, keepdims=True)
            l_new.append(tot)
        pv = [_dot(vt_ref[e, jj], jnp.concatenate([p_u.astype(BF16) for p_u in ps[e]], axis=0)) for e in heads]
        acc_new = [alpha[e] * acc[e] + pv[e] for e in heads]
        return tuple(m_new), tuple(l_new), tuple(acc_new)

    m, l, acc = lax.fori_loop(0, (i + group - 1) // group, step, (tuple(m), tuple(l), tuple(acc)))
    for e in heads:
        o_ref[:, e * dh:(e + 1) * dh] = (acc[e] * (1.0 / l[e])).T.astype(o_ref.dtype)


def _moba_prompt(q, k, v, n_heads, cache_k, page_table):
    t = q.shape[0]
    blk, dh = MB_BLOCK, MB_HEAD_DIM
    assert t % blk == 0
    nb = t // blk
    group = 4 if nb % 4 == 0 else 1
    hpb = 2 if n_heads % 2 == 0 else 1
    n_steps = (n_heads // hpb) * nb
    b, n_pages = page_table.shape
    _, _, page, c_heads, c_dh = cache_k.shape
    ppb = MB_BLOCK // PAGE_SIZE
    pages = (b * n_pages) // n_steps
    assert page == PAGE_SIZE and pages * n_steps == b * n_pages and pages % ppb == 0 and n_pages % pages == 0
    per_seq = n_pages // pages

    def page_map(e):
        return lambda h, i, pt: (0, pt[(h * nb + i) // per_seq, ((h * nb + i) % per_seq) * pages + e], 0, 0, 0)

    held = pl.BlockSpec((t, hpb * dh), lambda h, i, pt: (0, h), pipeline_mode=pl.Buffered(1))
    grid_spec = pltpu.PrefetchScalarGridSpec(
        num_scalar_prefetch=1,
        grid=(n_heads // hpb, nb),
        in_specs=[pl.BlockSpec((blk, hpb * dh), lambda h, i, pt: (i, h)), held, held]
                 + [pl.BlockSpec((1, 1, page, c_heads, c_dh), page_map(e)) for e in range(pages)],
        out_specs=[pl.BlockSpec((blk, hpb * dh), lambda h, i, pt: (i, h)),
                   pl.BlockSpec((1, n_pages // ppb, c_heads, c_dh), lambda h, i, pt: ((h * nb + i) // per_seq, 0, 0, 0))],
        scratch_shapes=[pltpu.VMEM((hpb, nb // group, group * blk, dh), BF16),
                        pltpu.VMEM((hpb, nb // group, dh, group * blk), BF16),
                        pltpu.VMEM((hpb, nb, dh, blk), BF16),
                        pltpu.VMEM((hpb, nb, dh), F32), pltpu.VMEM((hpb, nb, blk), F32)],
    )
    return pl.pallas_call(
        functools.partial(_moba_prompt_body, nb=nb, group=group, hpb=hpb, pages=pages),
        grid_spec=grid_spec,
        out_shape=[jax.ShapeDtypeStruct((t, n_heads * dh), BF16),
                   jax.ShapeDtypeStruct((b, n_pages // ppb, c_heads, c_dh), F32)],
        compiler_params=_params(2),
        name="moba_prompt",
    )(page_table, q, k, v, *([cache_k] * pages))


def _sample_gate_body(q_ref, km_ref, o_ref):
    km = km_ref[0]
    nblk, n_heads, dh = km.shape
    prod = (km * q_ref[0]).reshape(nblk * n_heads, dh)
    ones = jnp.ones((dh, dh), BF16)
    hi, mid, lo = _split3(prod)
    gs = (_dot(hi, ones) + _dot(mid, ones) + _dot(lo, ones)).reshape(nblk, n_heads, dh)
    _, order = _top_blocks(gs, nblk)
    order.append(jnp.zeros((8 - MB_TOPK, n_heads, dh), F32))
    o_ref[0] = jnp.concatenate(order, axis=0).astype(jnp.int32)


def _sample_gate(q_s3, kmean):
    b, nblk, n_heads, dh = kmean.shape
    return pl.pallas_call(
        _sample_gate_body,
        grid=(b,),
        in_specs=[pl.BlockSpec((1, n_heads, dh), lambda s: (s, 0, 0)),
                  pl.BlockSpec((1, nblk, n_heads, dh), lambda s: (s, 0, 0, 0))],
        out_specs=pl.BlockSpec((1, 8, n_heads, dh), lambda s: (s, 0, 0, 0)),
        out_shape=jax.ShapeDtypeStruct((b, 8, n_heads, dh), jnp.int32),
        compiler_params=_params(1),
        name="moba_sample_gate",
    )(q_s3, kmean)


def _sample_attn_body(pt_ref, top_ref, q_ref, kn_ref, vn_ref, ck_ref, cv_ref, o_ref, kbuf, vbuf, sem, *, n_heads):
    s = pl.program_id(0)
    ppb = MB_BLOCK // PAGE_SIZE
    n_pg = MB_TOPK * ppb
    dh = MB_HEAD_DIM

    def copies(seq, slot):
        out = []
        for h in range(n_heads):
            for t in range(MB_TOPK):
                blk_id = top_ref[seq, h * MB_TOPK + t]
                for e in range(ppb):
                    page = pt_ref[seq, ppb * blk_id + e]
                    out.append(pltpu.make_async_copy(ck_ref.at[0, page, :, h, :], kbuf.at[slot, h, t * ppb + e],
                                                     sem.at[0, slot]))
                    out.append(pltpu.make_async_copy(cv_ref.at[0, page, :, h, :], vbuf.at[slot, h, t * ppb + e],
                                                     sem.at[1, slot]))
        return out

    slot = s % 2

    @pl.when(s == 0)
    def _():
        for cp in copies(0, 0):
            cp.start()

    @pl.when(s + 1 < pl.num_programs(0))
    def _():
        for cp in copies(s + 1, 1 - slot):
            cp.start()

    for cp in copies(s, slot):
        cp.wait()

    scale = dh ** -0.5
    for h in range(n_heads):
        q = q_ref[0, h:h + 1, :] * scale
        keys = kbuf[slot, h].reshape(n_pg * PAGE_SIZE, dh).astype(BF16)
        vals = vbuf[slot, h].reshape(n_pg * PAGE_SIZE, dh).astype(BF16)
        sc = _dot_nt(jnp.broadcast_to(q, (8, dh)).astype(BF16), keys)[0:1, :]
        s_self = jnp.sum(q * kn_ref[0, h:h + 1, :], axis=-1, keepdims=True)
        m = jnp.maximum(jnp.max(sc, axis=-1, keepdims=True), s_self)
        p = jnp.exp(sc - m)
        p_self = jnp.exp(s_self - m)
        l = jnp.sum(p, axis=-1, keepdims=True) + p_self
        pv = _dot(jnp.broadcast_to(p, (8, p.shape[1])).astype(BF16), vals)[0:1, :]
        o_ref[0, h:h + 1, :] = (pv + vn_ref[0, h:h + 1, :] * p_self) * (1.0 / l)


def _sample_attn(q_s3, k_s3, v_s3, cache_k, cache_v, page_table, top, n_heads):
    b = q_s3.shape[0]
    dh = MB_HEAD_DIM
    n_pg = MB_TOPK * (MB_BLOCK // PAGE_SIZE)
    row = pl.BlockSpec((1, n_heads, dh), lambda s, pt, tp: (s, 0, 0))
    hbm = pl.BlockSpec(memory_space=pl.ANY)
    grid_spec = pltpu.PrefetchScalarGridSpec(
        num_scalar_prefetch=2,
        grid=(b,),
        in_specs=[row, row, row, hbm, hbm],
        out_specs=row,
        scratch_shapes=[pltpu.VMEM((2, n_heads, n_pg, PAGE_SIZE, dh), F32),
                        pltpu.VMEM((2, n_heads, n_pg, PAGE_SIZE, dh), F32),
                        pltpu.SemaphoreType.DMA((2, 2))],
    )
    return pl.pallas_call(
        functools.partial(_sample_attn_body, n_heads=n_heads),
        grid_spec=grid_spec,
        out_shape=jax.ShapeDtypeStruct((b, n_heads, dh), F32),
        compiler_params=_params(1),
        name="moba_sample_attn",
    )(page_table, top, q_s3, k_s3, v_s3, cache_k, cache_v)


def _layer(xp, xs, cache_k, cache_v, wkv_s, shift_s, page_table, norm_mix, w_in, rwkv_mu, w0, w2, a0, a2, g2,
           k_k, k_a, r_k, lnx_g, lnx_b, up_rw, up_mb, w_o, norm_ffn, w_gate, w_up, w_down):
    t, d_model = xp.shape
    b = xs.shape[0]
    n_rw_heads = r_k.shape[0]
    rw_width = n_rw_heads * RW_HEAD_DIM
    n_mb_heads = cache_k.shape[3]
    mb_width = n_mb_heads * MB_HEAD_DIM
    rw_proj = rwkv_mu.shape[0]
    lora = DECAY_LORA + ICLR_LORA + GATE_LORA
    assert rw_proj == 3 * rw_width + lora
    rw_pad = 3 * rw_width + LORA_TAIL
    rw_col0 = 3 * mb_width
    gate_col0 = rw_col0 + rw_proj
    d_ff = w_gate.shape[1]

    xn_p = _rmsnorm(xp, norm_mix, BF16)
    xn_s = _rmsnorm(xs, norm_mix, BF16)

    w_in_t = jnp.swapaxes(w_in, 0, 1)

    def proj(name, col0, n_out, tn, ep=_ep_identity, dt=F32):
        (o_p,), (o_s,) = _fused_matmul(name, [xn_p], [xn_s], [w_in_t], [col0], n_out, _pick_tile(n_out, tn), 1024,
                                       [], ep, [dt], w_t=True)
        return o_p, o_s

    q_p, q_s = proj("proj_q", 0, mb_width, 512)
    k_p, k_s = proj("proj_k", mb_width, mb_width, 512)
    v_p, v_s = proj("proj_v", 2 * mb_width, mb_width, 512)
    p_rw_p, p_rw_s = proj("proj_rw", rw_col0, rw_pad, 512)
    sg_p, sg_s = proj("proj_gates", gate_col0, 2 * d_model, 512, ep=_ep_sigmoid, dt=BF16)

    pad_row = lambda a: jnp.pad(a, (0, rw_pad - rw_proj)).reshape(1, rw_pad)
    lora_pad = lambda a, r0: jnp.pad(a, ((r0, LORA_TAIL - r0 - a.shape[0]), (0, 0))).astype(BF16)
    prm = dict(
        mu=pad_row(rwkv_mu), w0=w0.reshape(1, rw_width), a0=a0.reshape(1, rw_width),
        k_k=k_k.reshape(1, rw_width), k_a=k_a.reshape(1, rw_width), r_k=r_k.reshape(1, rw_width),
        lnx_g=lnx_g.reshape(1, rw_width), lnx_b=lnx_b.reshape(1, rw_width),
        w2p=lora_pad(w2, 0), a2p=lora_pad(a2, DECAY_LORA), g2p=lora_pad(g2, DECAY_LORA + ICLR_LORA))
    y_rw_p, st_pairs = _rw_scan_prompt(p_rw_p, prm, rw_width)
    n_pair = rw_width // LANES
    st5 = st_pairs.reshape(n_pair, 2, RW_HEAD_DIM, 2, RW_HEAD_DIM)
    wkv_p = jnp.stack([st5[:, 0, :, 0, :], st5[:, 1, :, 1, :]], axis=1)
    wkv_p = jnp.swapaxes(wkv_p, -1, -2).reshape(n_rw_heads, RW_HEAD_DIM, RW_HEAD_DIM)
    shift_p = p_rw_p[t - 1, :rw_proj]

    prev_s = jnp.pad(shift_s, ((0, 0), (0, rw_pad - rw_proj)))
    rows = _rw_sample_prep(p_rw_s, prev_s, prm, rw_width)
    rows = [a.reshape(b, n_rw_heads, 1, RW_HEAD_DIM) for a in rows]
    head_prm = [a.reshape(n_rw_heads, 1, RW_HEAD_DIM) for a in (k_k, k_a, r_k, lnx_g, lnx_b)]
    wkv_s_new, y_rw_s = _rw_sample_step(wkv_s, rows, head_prm)
    y_rw_s = y_rw_s.reshape(b, rw_width).astype(BF16)
    shift_s_new = p_rw_s[:, :rw_proj]

    y_mb_p, kmean_s = _moba_prompt(q_p, k_p, v_p, n_mb_heads, cache_k, page_table)
    q_s3 = q_s.reshape(b, n_mb_heads, MB_HEAD_DIM)
    top = _sample_gate(q_s3, kmean_s)
    top = jnp.swapaxes(top[:, :MB_TOPK, :, 0], 1, 2).reshape(b, n_mb_heads * MB_TOPK)
    y_mb_s = _sample_attn(q_s3, k_s.reshape(b, n_mb_heads, MB_HEAD_DIM), v_s.reshape(b, n_mb_heads, MB_HEAD_DIM),
                          cache_k, cache_v, page_table, top, n_mb_heads)
    y_mb_s = y_mb_s.reshape(b, mb_width).astype(BF16)

    (mg_p,), (mg_s,) = _fused_matmul("merge", [y_rw_p, y_mb_p], [y_rw_s, y_mb_s], [up_rw, up_mb], [0, 0],
                                     d_model, _pick_tile(d_model, 512), 1024,
                                     [(sg_p, sg_s, 0), (sg_p, sg_s, d_model)], _ep_merge, [BF16])
    (x2_p,), (x2_s,) = _fused_matmul("out_proj", [mg_p], [mg_s], [w_o], [0], d_model, _pick_tile(d_model, 512), 1024,
                                     [(xp, xs, 0)], _ep_residual, [F32])
    xn2_p = _rmsnorm(x2_p, norm_ffn, BF16)
    xn2_s = _rmsnorm(x2_s, norm_ffn, BF16)
    (h_p,), (h_s,) = _fused_matmul("ffn_up", [xn2_p], [xn2_s], [w_gate, w_up], [0, 0],
                                   d_ff, _pick_tile(d_ff, 256), 1024, [], _ep_swiglu, [BF16], x_of=[0, 0])
    (x3_p,), (x3_s,) = _fused_matmul("ffn_down", [h_p], [h_s], [w_down.astype(BF16)], [0], d_model,
                                     _pick_tile(d_model, 512), 512, [(x2_p, x2_s, 0)], _ep_residual, [F32])
    return (x3_p, x3_s, k_p, v_p, wkv_p, shift_p, k_s, v_s, wkv_s_new, shift_s_new)


def kernel(x_prompt, x_sample, cache_k, cache_v, state_wkv, state_shift, page_table, norm_mix, w_in, rwkv_mu, w0, w2, a0, a2, g2, k_k, k_a, r_k, lnx_g, lnx_b, up_rw, up_mb, w_o, norm_ffn, w_gate, w_up, w_down, norm_final):
    depth = w_in.shape[0]
    assert depth == 1 and x_prompt.shape[0] == 1 and x_sample.shape[1] == 1
    t, d_model = x_prompt.shape[1], x_prompt.shape[2]
    b = x_sample.shape[0]
    n_mb_heads = cache_k.shape[3]
    n_rw_heads = r_k.shape[1]
    out = _layer(x_prompt[0], x_sample[:, 0], cache_k, cache_v, state_wkv[0], state_shift[0], page_table,
                 norm_mix[0], w_in[0], rwkv_mu[0], w0[0], w2[0], a0[0], a2[0], g2[0], k_k[0], k_a[0], r_k[0],
                 lnx_g[0], lnx_b[0], up_rw[0], up_mb[0], w_o[0], norm_ffn[0], w_gate[0], w_up[0], w_down[0])
    x3_p, x3_s, k_p, v_p, wkv_p, shift_p, k_s, v_s, wkv_s, shift_s = out
    y_p = _rmsnorm(x3_p, norm_final, F32)
    y_s = _rmsnorm(x3_s, norm_final, F32)
    rw_proj = rwkv_mu.shape[1]
    return (y_p.reshape(1, t, d_model), y_s.reshape(b, 1, d_model),
            k_p.reshape(1, 1, t, n_mb_heads, MB_HEAD_DIM), v_p.reshape(1, 1, t, n_mb_heads, MB_HEAD_DIM),
            wkv_p.reshape(1, 1, n_rw_heads, RW_HEAD_DIM, RW_HEAD_DIM), shift_p.reshape(1, 1, rw_proj),
            k_s.reshape(1, b, 1, n_mb_heads, MB_HEAD_DIM), v_s.reshape(1, b, 1, n_mb_heads, MB_HEAD_DIM),
            wkv_s.reshape(1, b, n_rw_heads, RW_HEAD_DIM, RW_HEAD_DIM), shift_s.reshape(1, b, rw_proj))
```

```python
import functools

import jax
import jax.numpy as jnp
from jax import lax
from jax.experimental import pallas as pl
from jax.experimental.pallas import tpu as pltpu

F32 = jnp.float32
BF16 = jnp.bfloat16

RMS_EPS = 1e-6
LNX_EPS = 64e-5
MB_BLOCK = 256
MB_TOPK = 3
MB_HEAD_DIM = 128
PAGE_SIZE = 128
RW_HEAD_DIM = 64
DECAY_LORA = 96
ICLR_LORA = 96
GATE_LORA = 256
LORA_TAIL = 512
SCAN_CHUNK = 64
LANES = 128
SUBLANES = 8
VMEM_LIMIT = 56 * 1024 * 1024

TILE_WIDE = (1024, 512)
TILE_TWO_WEIGHTS = (1024, 256)
TILE_LONG_K = (512, 512)

_NEG_INF = float("-inf")
_LOG2E = 1.4426950408889634


def _params(n_axes):
    return pltpu.CompilerParams(dimension_semantics=("arbitrary",) * n_axes,
                                vmem_limit_bytes=VMEM_LIMIT)


def _dot(a, b):
    return jnp.dot(a, b, preferred_element_type=F32)


def _dot_nt(a, b):
    return lax.dot_general(a, b, (((1,), (1,)), ((), ())), preferred_element_type=F32)


def _dot_tn(a, b):
    return lax.dot_general(a, b, (((0,), (0,)), ((), ())), preferred_element_type=F32)


def _split2(x):
    hi = x.astype(BF16)
    lo = (x - hi.astype(F32)).astype(BF16)
    return hi, lo


def _split3(x):
    hi = x.astype(BF16)
    r1 = x - hi.astype(F32)
    mid = r1.astype(BF16)
    lo = (r1 - mid.astype(F32)).astype(BF16)
    return hi, mid, lo


def _sigmoid(x):
    return 1.0 / (1.0 + jnp.exp(-x))


def _softplus(x):
    return jnp.maximum(x, 0.0) + jnp.log1p(jnp.exp(-jnp.abs(x)))


def _rmsnorm_body(x_ref, g_ref, o_ref):
    x = x_ref[...]
    ms = jnp.mean(x * x, axis=-1, keepdims=True)
    o_ref[...] = (x * lax.rsqrt(ms + RMS_EPS) * g_ref[...]).astype(o_ref.dtype)


def _rmsnorm(x, g, out_dtype, tm=256):
    m, d = x.shape
    tm = min(tm, m)
    assert m % tm == 0
    return pl.pallas_call(
        _rmsnorm_body,
        grid=(m // tm,),
        in_specs=[pl.BlockSpec((tm, d), lambda i: (i, 0)), pl.BlockSpec((1, d), lambda i: (0, 0))],
        out_specs=pl.BlockSpec((tm, d), lambda i: (i, 0)),
        out_shape=jax.ShapeDtypeStruct((m, d), out_dtype),
        compiler_params=_params(1),
        name="rmsnorm",
    )(x, g.reshape(1, d))


def _pick_tile(n, pref):
    t = pref
    while n % t:
        t //= 2
    assert t >= LANES
    return t


def _fused_matmul(name, xp, xs, ws, w_col0, n_out, tn, tm, extras, epilogue, out_dtypes, x_of=None, w_t=False):
    nd, ne, no, nx = len(ws), len(extras), len(out_dtypes), len(xp)
    mm = _dot_nt if w_t else _dot
    x_of = list(range(nd)) if x_of is None else x_of
    mp, ms = xp[0].shape[0], xs[0].shape[0]
    tm = min(tm, mp)
    assert mp % tm == 0 and n_out % tn == 0
    nj, ni = n_out // tn, mp // tm
    cast = [w.dtype != BF16 for w in ws]
    n_cast = sum(cast)
    cast_slot = [sum(cast[:k]) for k in range(nd)]

    def body(*refs):
        xp_r = refs[:nx]
        xs_r = refs[nx:2 * nx]
        w_r = refs[2 * nx:2 * nx + nd]
        base = 2 * nx + nd
        ep_r = refs[base:base + ne]
        es_r = refs[base + ne:base + 2 * ne]
        base += 2 * ne
        op_r = refs[base:base + no]
        os_r = refs[base + no:base + 2 * no]
        wb_r = refs[base + 2 * no:]
        i = pl.program_id(1)

        def weight(k):
            return wb_r[cast_slot[k]][...] if cast[k] else w_r[k][...]

        @pl.when(i == 0)
        def _():
            for k in range(nd):
                if cast[k]:
                    wb_r[cast_slot[k]][...] = w_r[k][...].astype(BF16)
            accs = [mm(xs_r[x_of[k]][...], weight(k)) for k in range(nd)]
            outs = epilogue(accs, [e[...] for e in es_r])
            for o, val in zip(os_r, outs):
                o[...] = val.astype(o.dtype)

        accs = [mm(xp_r[x_of[k]][...], weight(k)) for k in range(nd)]
        outs = epilogue(accs, [e[...] for e in ep_r])
        for o, val in zip(op_r, outs):
            o[...] = val.astype(o.dtype)

    in_specs = []
    for k in range(nx):
        kk = xp[k].shape[1]
        in_specs.append(pl.BlockSpec((tm, kk), lambda j, i: (i, 0)))
    for k in range(nx):
        kk = xs[k].shape[1]
        in_specs.append(pl.BlockSpec((ms, kk), lambda j, i: (0, 0)))
    for k in range(nd):
        c0 = w_col0[k]
        if not w_t:
            assert c0 % tn == 0
            in_specs.append(pl.BlockSpec((ws[k].shape[0], tn), functools.partial(lambda j, i, c: (0, j + c), c=c0 // tn)))
        elif c0 % tn == 0:
            in_specs.append(pl.BlockSpec((tn, ws[k].shape[1]), functools.partial(lambda j, i, c: (j + c, 0), c=c0 // tn)))
        else:
            assert c0 % SUBLANES == 0
            in_specs.append(pl.BlockSpec((pl.Element(tn), pl.Element(ws[k].shape[1])),
                                         functools.partial(lambda j, i, c: (pl.multiple_of(c + j * tn, SUBLANES), 0), c=c0)))
    for (_, _, c0) in extras:
        assert c0 % tn == 0
        in_specs.append(pl.BlockSpec((tm, tn), functools.partial(lambda j, i, c: (i, j + c), c=c0 // tn)))
    for (_, _, c0) in extras:
        in_specs.append(pl.BlockSpec((ms, tn), functools.partial(lambda j, i, c: (0, j + c), c=c0 // tn)))
    out_specs = [pl.BlockSpec((tm, tn), lambda j, i: (i, j)) for _ in range(no)]
    out_specs += [pl.BlockSpec((ms, tn), lambda j, i: (0, j)) for _ in range(no)]
    out_shape = [jax.ShapeDtypeStruct((mp, n_out), dt) for dt in out_dtypes]
    out_shape += [jax.ShapeDtypeStruct((ms, n_out), dt) for dt in out_dtypes]
    scratch = [pltpu.VMEM((tn, ws[k].shape[1]) if w_t else (ws[k].shape[0], tn), BF16) for k in range(nd) if cast[k]]
    assert len(scratch) == n_cast
    args = list(xp) + list(xs) + list(ws) + [e[0] for e in extras] + [e[1] for e in extras]
    outs = pl.pallas_call(
        body,
        grid=(nj, ni),
        in_specs=in_specs,
        out_specs=out_specs,
        out_shape=out_shape,
        scratch_shapes=scratch,
        compiler_params=_params(2),
        name=name,
    )(*args)
    return outs[:no], outs[no:]


def _ep_identity(accs, extras):
    return (accs[0],)


def _ep_sigmoid(accs, extras):
    return (_sigmoid(accs[0]),)


def _ep_merge(accs, extras):
    return (extras[0].astype(F32) * accs[0] + extras[1].astype(F32) * accs[1],)


def _ep_residual(accs, extras):
    return (extras[0] + accs[0],)


def _ep_swiglu(accs, extras):
    g = accs[0]
    return (g * _sigmoid(g) * accs[1],)


def _rw_prep(p, prev, mu, w0, a0, w2p, a2p, g2p, width):
    pm = p + (prev - p) * mu
    r = pm[:, :width]
    k = pm[:, width:2 * width]
    v = pm[:, 2 * width:3 * width]
    tail = pm[:, 3 * width:3 * width + LORA_TAIL]
    lw = w0 + _dot(jnp.tanh(tail).astype(BF16), w2p)
    w_log = -_softplus(-lw) - 0.5
    logd = -jnp.exp(w_log)
    a_lr = _sigmoid(a0 + _dot(tail.astype(BF16), a2p))
    g = _dot(_sigmoid(tail).astype(BF16), g2p)
    return r, k, v, logd, a_lr, g


def _seg_ones():
    ri = lax.broadcasted_iota(jnp.int32, (LANES, LANES), 0) // RW_HEAD_DIM
    ci = lax.broadcasted_iota(jnp.int32, (LANES, LANES), 1) // RW_HEAD_DIM
    return jnp.where(ri == ci, 1.0, 0.0).astype(BF16)


def _seg_sum(x, ones_bd):
    return _dot(x.astype(BF16), ones_bd)


def _rw_scan_body(p_ref, mu_ref, w0_ref, a0_ref, kk_ref, ka_ref, rk_ref, lg_ref, lb_ref,
                  w2_ref, a2_ref, g2_ref, y_ref, st_ref, pbuf_ref, state_ref, *, width):
    c = pl.program_id(0)
    ch = SCAN_CHUNK
    n_pair = width // LANES

    @pl.when(c == 0)
    def _():
        pbuf_ref[...] = jnp.zeros_like(pbuf_ref)
        state_ref[...] = jnp.zeros_like(state_ref)

    pbuf_ref[pl.ds(SUBLANES, ch), :] = p_ref[...]
    p = pbuf_ref[pl.ds(SUBLANES, ch), :]
    prev = pbuf_ref[pl.ds(SUBLANES - 1, ch), :]
    r, k, v, logd, a_lr, g = _rw_prep(p, prev, mu_ref[...], w0_ref[...], a0_ref[...],
                                      w2_ref[...], a2_ref[...], g2_ref[...], width)
    pbuf_ref[pl.ds(0, SUBLANES), :] = pbuf_ref[pl.ds(ch, SUBLANES), :]

    ti = lax.broadcasted_iota(jnp.int32, (ch, ch), 0)
    tj = lax.broadcasted_iota(jnp.int32, (ch, ch), 1)
    ltri = jnp.where(ti >= tj, 1.0, 0.0).astype(BF16)
    d_hi, d_mid, d_lo = _split3(logd)
    lc = _dot(ltri, d_hi) + _dot(ltri, d_mid) + _dot(ltri, d_lo)
    lc_end = lc[ch - 1:ch, :]
    w_t = jnp.exp(lc)
    w_tm1 = jnp.exp(lc - logd)
    w_inv = jnp.exp(-lc)
    w_rem = jnp.exp(lc_end - lc)
    w_end = jnp.exp(lc_end)

    ones_bd = _seg_ones()
    kkr = k * kk_ref[...]
    kf = k * (1.0 + (a_lr - 1.0) * ka_ref[...])
    rkk = r * kf * rk_ref[...]

    lane = lax.broadcasted_iota(jnp.int32, (ch, LANES), 1)
    m0 = jnp.where(lane < RW_HEAD_DIM, 1.0, 0.0)
    m1 = 1.0 - m0

    def stack(z):
        return jnp.concatenate([z * m0, z * m1], axis=0)

    ri = lax.broadcasted_iota(jnp.int32, (2 * ch, 2 * ch), 0) % ch
    ci = lax.broadcasted_iota(jnp.int32, (2 * ch, 2 * ch), 1) % ch
    tri_strict = jnp.where(ri > ci, 1.0, 0.0)
    tri_incl = jnp.where(ri >= ci, 1.0, 0.0)
    er = lax.broadcasted_iota(jnp.int32, (LANES, LANES), 0)
    ec = lax.broadcasted_iota(jnp.int32, (LANES, LANES), 1)
    eye = jnp.where(er == ec, 1.0, 0.0)

    pairs = range(n_pair)
    sls = [slice(gp * LANES, (gp + 1) * LANES) for gp in pairs]
    h2 = 2 * ch

    def rows_of(z_list):
        return jnp.concatenate(z_list, axis=0)

    kkr_g = [kkr[:, sl] for sl in sls]
    ss = _seg_sum(rows_of([z * z for z in kkr_g]), ones_bd)
    kk_g = [kkr_g[gp] / jnp.maximum(jnp.sqrt(ss[gp * ch:(gp + 1) * ch]), 1e-12) for gp in pairs]
    braw_g = [kk_g[gp] * a_lr[:, sls[gp]] for gp in pairs]
    kf_g = [kf[:, sl] for sl in sls]
    a_s = [stack(-kk_g[gp] * w_tm1[:, sls[gp]]).astype(BF16) for gp in pairs]
    r_s = [stack(r[:, sls[gp]] * w_t[:, sls[gp]]).astype(BF16) for gp in pairs]
    v_s = [stack(v[:, sl]).astype(BF16) for sl in sls]
    rhs = [jnp.concatenate([stack(braw_g[gp] * w_inv[:, sls[gp]]), stack(kf_g[gp] * w_inv[:, sls[gp]])],
                           axis=0).astype(BF16) for gp in pairs]
    upd_l = [jnp.concatenate([stack(braw_g[gp] * w_rem[:, sls[gp]]), stack(kf_g[gp] * w_rem[:, sls[gp]])],
                             axis=0).astype(BF16) for gp in pairs]
    gram = [_dot_nt(jnp.concatenate([a_s[gp], r_s[gp]], axis=0), rhs[gp]) for gp in pairs]
    pw = [(gram[gp][:h2, :h2] * tri_strict).astype(BF16) for gp in pairs]
    n_bd = [(gram[gp][:h2, h2:] * tri_strict).astype(BF16) for gp in pairs]
    pb = [(gram[gp][h2:, :h2] * tri_incl).astype(BF16) for gp in pairs]
    pk = [(gram[gp][h2:, h2:] * tri_incl).astype(BF16) for gp in pairs]

    st = [state_ref[gp] for gp in pairs]
    st_b = [z.astype(BF16) for z in st]
    sa = [_dot(jnp.concatenate([a_s[gp], n_bd[gp]], axis=1), jnp.concatenate([st_b[gp], v_s[gp]], axis=0))
          for gp in pairs]
    n_steps = ch.bit_length() - 1
    for s in range(n_steps):
        sa = [sa[gp] + _dot(pw[gp], sa[gp].astype(BF16)) for gp in pairs]
        if s + 1 < n_steps:
            pw = [_dot(pw[gp], pw[gp]).astype(BF16) for gp in pairs]
    sa_b = [z.astype(BF16) for z in sa]
    y_s = [_dot(jnp.concatenate([r_s[gp], pb[gp], pk[gp]], axis=1),
                jnp.concatenate([st_b[gp], sa_b[gp], v_s[gp]], axis=0)) for gp in pairs]
    upd = [_dot_tn(upd_l[gp], jnp.concatenate([sa_b[gp], v_s[gp]], axis=0)) for gp in pairs]
    for gp in pairs:
        w_col = jnp.sum(eye * w_end[:, sls[gp]], axis=1, keepdims=True)
        state_ref[gp] = st[gp] * w_col + upd[gp]

    y = rows_of([y_s[gp][:ch] + y_s[gp][ch:] for gp in pairs])
    mean = _seg_sum(y, ones_bd) * (1.0 / RW_HEAD_DIM)
    d = y - mean
    var = _seg_sum(d * d, ones_bd) * (1.0 / RW_HEAD_DIM)
    yn = d * lax.rsqrt(var + LNX_EPS)
    bonus = _seg_sum(rows_of([rkk[:, sl] for sl in sls]), ones_bd)
    for gp in pairs:
        sl = sls[gp]
        rs = slice(gp * ch, (gp + 1) * ch)
        out = (yn[rs] * lg_ref[:, sl] + lb_ref[:, sl] + bonus[rs] * v[:, sl]) * g[:, sl]
        y_ref[:, sl] = out.astype(y_ref.dtype)

    @pl.when(c == pl.num_programs(0) - 1)
    def _():
        st_ref[...] = state_ref[...]


def _rw_scan_prompt(p_rw, prm, width):
    t, pw = p_rw.shape
    ch = SCAN_CHUNK
    assert t % ch == 0 and width % LANES == 0
    n_pair = width // LANES
    row = lambda n: pl.BlockSpec((1, n), lambda c: (0, 0))
    full = lambda a: pl.BlockSpec(a.shape, lambda c: (0, 0))
    return pl.pallas_call(
        functools.partial(_rw_scan_body, width=width),
        grid=(t // ch,),
        in_specs=[pl.BlockSpec((ch, pw), lambda c: (c, 0)), row(pw)] + [row(width)] * 7
                 + [full(prm["w2p"]), full(prm["a2p"]), full(prm["g2p"])],
        out_specs=[pl.BlockSpec((ch, width), lambda c: (c, 0)),
                   pl.BlockSpec((n_pair, LANES, LANES), lambda c: (0, 0, 0))],
        out_shape=[jax.ShapeDtypeStruct((t, width), BF16),
                   jax.ShapeDtypeStruct((n_pair, LANES, LANES), F32)],
        scratch_shapes=[pltpu.VMEM((ch + SUBLANES, pw), F32), pltpu.VMEM((n_pair, LANES, LANES), F32)],
        compiler_params=_params(1),
        name="rwkv_scan",
    )(p_rw, prm["mu"], prm["w0"], prm["a0"], prm["k_k"], prm["k_a"], prm["r_k"], prm["lnx_g"], prm["lnx_b"],
      prm["w2p"], prm["a2p"], prm["g2p"])


def _rw_sample_prep_body(p_ref, prev_ref, mu_ref, w0_ref, a0_ref, w2_ref, a2_ref, g2_ref,
                         r_ref, k_ref, v_ref, d_ref, alr_ref, g_ref, *, width):
    r, k, v, logd, a_lr, g = _rw_prep(p_ref[...], prev_ref[...], mu_ref[...], w0_ref[...], a0_ref[...],
                                      w2_ref[...], a2_ref[...], g2_ref[...], width)
    r_ref[...] = r
    k_ref[...] = k
    v_ref[...] = v
    d_ref[...] = logd
    alr_ref[...] = a_lr
    g_ref[...] = g


def _rw_sample_prep(p_s, prev_s, prm, width):
    b = p_s.shape[0]
    args = (p_s, prev_s, prm["mu"], prm["w0"], prm["a0"], prm["w2p"], prm["a2p"], prm["g2p"])
    return pl.pallas_call(
        functools.partial(_rw_sample_prep_body, width=width),
        grid=(1,),
        in_specs=[pl.BlockSpec(a.shape, lambda i: (0, 0)) for a in args],
        out_specs=[pl.BlockSpec((b, width), lambda i: (0, 0))] * 6,
        out_shape=[jax.ShapeDtypeStruct((b, width), F32)] * 6,
        compiler_params=_params(1),
        name="rwkv_sample_prep",
    )(*args)


def _rw_sample_step_body(s_ref, r_ref, k_ref, v_ref, d_ref, alr_ref, g_ref,
                         kk_ref, ka_ref, rk_ref, lg_ref, lb_ref, so_ref, y_ref):
    s = s_ref[0]
    r, k, v = r_ref[0], k_ref[0], v_ref[0]
    a_lr, g = alr_ref[0], g_ref[0]
    w = jnp.exp(d_ref[0])
    kkr = k * kk_ref[...]
    kk = kkr / jnp.maximum(jnp.sqrt(jnp.sum(kkr * kkr, axis=-1, keepdims=True)), 1e-12)
    kf = k * (1.0 + (a_lr - 1.0) * ka_ref[...])
    a = -kk
    b = kk * a_lr
    n = RW_HEAD_DIM
    eye = (lax.broadcasted_iota(jnp.int32, (n, n), 0) == lax.broadcasted_iota(jnp.int32, (n, n), 1))
    sa = jnp.sum(s * a, axis=-1, keepdims=True)
    v_col = jnp.sum(jnp.where(eye, v, 0.0), axis=-1, keepdims=True)
    s1 = s * w + sa * b + v_col * kf
    so_ref[0] = s1
    y_col = jnp.sum(s1 * r, axis=-1, keepdims=True)
    y = jnp.sum(jnp.where(eye, y_col, 0.0), axis=-2, keepdims=True)
    mean = jnp.mean(y, axis=-1, keepdims=True)
    d = y - mean
    var = jnp.mean(d * d, axis=-1, keepdims=True)
    yn = d * lax.rsqrt(var + LNX_EPS) * lg_ref[...] + lb_ref[...]
    bonus = jnp.sum(r * kf * rk_ref[...], axis=-1, keepdims=True) * v
    y_ref[0] = (yn + bonus) * g


def _rw_sample_step(state, rows, head_prm):
    b, h, n, _ = state.shape
    vec = pl.BlockSpec((1, h, 1, n), lambda i: (i, 0, 0, 0))
    par = pl.BlockSpec((h, 1, n), lambda i: (0, 0, 0))
    mat = pl.BlockSpec((1, h, n, n), lambda i: (i, 0, 0, 0))
    return pl.pallas_call(
        _rw_sample_step_body,
        grid=(b,),
        in_specs=[mat] + [vec] * 6 + [par] * 5,
        out_specs=[mat, vec],
        out_shape=[jax.ShapeDtypeStruct((b, h, n, n), F32), jax.ShapeDtypeStruct((b, h, 1, n), F32)],
        compiler_params=_params(1),
        name="rwkv_sample_step",
    )(state, *rows, *head_prm)


def _top_blocks(gs, n_valid):
    nb = gs.shape[0]
    rowi = lax.broadcasted_iota(jnp.int32, gs.shape, 0)
    rowf = rowi.astype(F32)
    gm = jnp.where(rowi < n_valid, gs, _NEG_INF)
    picked = jnp.zeros(gs.shape, F32)
    order = []
    for _ in range(MB_TOPK):
        mx = jnp.max(gm, axis=0, keepdims=True)
        first = jnp.min(jnp.where(gm == mx, rowf, float(nb)), axis=0, keepdims=True)
        hit = jnp.where(rowf == first, jnp.where(mx > _NEG_INF, 1.0, 0.0), 0.0)
        picked = picked + hit
        gm = jnp.where(hit > 0.0, _NEG_INF, gm)
        order.append(first)
    return picked, order


def _moba_prompt_body(pt_ref, q_ref, k_ref, v_ref, *refs, nb, group, hpb, pages):
    page_refs = refs[:pages]
    o_ref, pm_ref, kb_ref, vt_ref, vt1_ref, km_ref, bias_ref = refs[pages:]
    i = pl.program_id(1)
    blk, dh = MB_BLOCK, MB_HEAD_DIM
    heads = range(hpb)

    if pages:
        ppb = MB_BLOCK // PAGE_SIZE
        step = pl.program_id(0) * pl.num_programs(1) + i
        first = (step % (pm_ref.shape[1] * ppb // pages)) * (pages // ppb)
        for u in range(pages // ppb):
            tot = jnp.sum(page_refs[u * ppb][0, 0], axis=0)
            for e in range(1, ppb):
                tot = tot + jnp.sum(page_refs[u * ppb + e][0, 0], axis=0)
            pm_ref[0, first + u] = tot * (1.0 / MB_BLOCK)

    @pl.when(i == 0)
    def _():
        for e in heads:
            for n in range(nb):
                kn = k_ref[n * blk:(n + 1) * blk, e * dh:(e + 1) * dh]
                u = n % group
                kb_ref[e, n // group, u * blk:(u + 1) * blk, :] = kn.astype(BF16)
                km_ref[e, n:n + 1, :] = jnp.mean(kn, axis=0, keepdims=True)
                vn_t = v_ref[n * blk:(n + 1) * blk, e * dh:(e + 1) * dh].T.astype(BF16)
                vt_ref[e, n // group, :, u * blk:(u + 1) * blk] = vn_t
                vt1_ref[e, n] = vn_t

    q_t = [q_ref[:, e * dh:(e + 1) * dh].T for e in heads]
    for e in heads:
        q_hi, q_lo = _split2(q_t[e])
        k_hi, k_lo = _split2(km_ref[e])
        gs = _dot(k_hi, q_hi) + _dot(k_hi, q_lo) + _dot(k_lo, q_hi)
        picked, _ = _top_blocks(gs, i)
        bias_ref[e] = jnp.where(picked > 0.0, 0.0, _NEG_INF)

    q_s = [(q_t[e] * (dh ** -0.5 * _LOG2E)).astype(BF16) for e in heads]
    own = pl.multiple_of((i % group) * blk, blk)
    kr = lax.broadcasted_iota(jnp.int32, (blk, blk), 0)
    qc = lax.broadcasted_iota(jnp.int32, (blk, blk), 1)
    s = [jnp.where(kr <= qc, _dot(kb_ref[e, i // group, pl.ds(own, blk), :], q_s[e]), _NEG_INF)
         for e in heads]
    m = [jnp.max(s[e], axis=0, keepdims=True) for e in heads]
    p = [jnp.exp2(s[e] - m[e]) for e in heads]
    l = [jnp.sum(p[e], axis=0, keepdims=True) for e in heads]
    acc = [_dot(vt1_ref[e, i], p[e].astype(BF16)) for e in heads]

    def step(jj, carry):
        m, l, acc = carry
        s = [_dot(kb_ref[e, jj], q_s[e]) for e in heads]
        m_run = list(m)
        m_at, p_sum, p_bf = [[] for _ in heads], [[] for _ in heads], [[] for _ in heads]
        for u in range(group):
            for e in heads:
                t_u = s[e][u * blk:(u + 1) * blk] + bias_ref[e, pl.ds(jj * group + u, 1), :]
                m_run[e] = jnp.maximum(m_run[e], jnp.max(t_u, axis=0, keepdims=True))
                p_u = jnp.exp2(t_u - m_run[e])
                m_at[e].append(m_run[e])
                p_sum[e].append(jnp.sum(p_u, axis=0, keepdims=True))
                p_bf[e].append(p_u.astype(BF16))
        l_new, acc_new = [], []
        for e in heads:
            alpha = jnp.exp2(m[e] - m_run[e])
            tot = alpha * l[e]
            out = alpha * acc[e]
            for u in range(group):
                c_u = jnp.exp2(m_at[e][u] - m_run[e])
                tot = tot + c_u * p_sum[e][u]
                out = out + c_u * _dot(vt_ref[e, jj, :, u * blk:(u + 1) * blk], p_bf[e][u])
            l_new.append(tot)
            acc_new.append(out)
        return tuple(m_run), tuple(l_new), tuple(acc_new)

    m, l, acc = lax.fori_loop(0, (i + group - 1) // group, step, (tuple(m), tuple(l), tuple(acc)))
    for e in heads:
        o_ref[:, e * dh:(e + 1) * dh] = (acc[e] * (1.0 / l[e])).T.astype(o_ref.dtype)


def _moba_prompt(q, k, v, n_heads, cache_k, page_table):
    t = q.shape[0]
    blk, dh = MB_BLOCK, MB_HEAD_DIM
    assert t % blk == 0
    nb = t // blk
    group = 4 if nb % 4 == 0 else 1
    hpb = 2 if n_heads % 2 == 0 else 1
    n_steps = (n_heads // hpb) * nb
    b, n_pages = page_table.shape
    _, _, page, c_heads, c_dh = cache_k.shape
    ppb = MB_BLOCK // PAGE_SIZE
    pages = (b * n_pages) // n_steps
    assert page == PAGE_SIZE and pages * n_steps == b * n_pages and pages % ppb == 0 and n_pages % pages == 0
    per_seq = n_pages // pages

    def page_map(e):
        return lambda h, i, pt: (0, pt[(h * nb + i) // per_seq, ((h * nb + i) % per_seq) * pages + e], 0, 0, 0)

    held = pl.BlockSpec((t, hpb * dh), lambda h, i, pt: (0, h), pipeline_mode=pl.Buffered(1))
    grid_spec = pltpu.PrefetchScalarGridSpec(
        num_scalar_prefetch=1,
        grid=(n_heads // hpb, nb),
        in_specs=[pl.BlockSpec((blk, hpb * dh), lambda h, i, pt: (i, h)), held, held]
                 + [pl.BlockSpec((1, 1, page, c_heads, c_dh), page_map(e)) for e in range(pages)],
        out_specs=[pl.BlockSpec((blk, hpb * dh), lambda h, i, pt: (i, h)),
                   pl.BlockSpec((1, n_pages // ppb, c_heads, c_dh), lambda h, i, pt: ((h * nb + i) // per_seq, 0, 0, 0))],
        scratch_shapes=[pltpu.VMEM((hpb, nb // group, group * blk, dh), BF16),
                        pltpu.VMEM((hpb, nb // group, dh, group * blk), BF16),
                        pltpu.VMEM((hpb, nb, dh, blk), BF16),
                        pltpu.VMEM((hpb, nb, dh), F32), pltpu.VMEM((hpb, nb, blk), F32)],
    )
    return pl.pallas_call(
        functools.partial(_moba_prompt_body, nb=nb, group=group, hpb=hpb, pages=pages),
        grid_spec=grid_spec,
        out_shape=[jax.ShapeDtypeStruct((t, n_heads * dh), BF16),
                   jax.ShapeDtypeStruct((b, n_pages // ppb, c_heads, c_dh), F32)],
        compiler_params=_params(2),
        name="moba_prompt",
    )(page_table, q, k, v, *([cache_k] * pages))


def _sample_gate_body(q_ref, km_ref, o_ref):
    km = km_ref[0]
    nblk, n_heads, dh = km.shape
    prod = (km * q_ref[0]).reshape(nblk * n_heads, dh)
    ones = jnp.ones((dh, dh), BF16)
    hi, mid, lo = _split3(prod)
    gs = (_dot(hi, ones) + _dot(mid, ones) + _dot(lo, ones)).reshape(nblk, n_heads, dh)
    _, order = _top_blocks(gs, nblk)
    order.append(jnp.zeros((SUBLANES - MB_TOPK, n_heads, dh), F32))
    o_ref[0] = jnp.concatenate(order, axis=0).astype(jnp.int32)


def _sample_gate(q_s3, kmean):
    b, nblk, n_heads, dh = kmean.shape
    return pl.pallas_call(
        _sample_gate_body,
        grid=(b,),
        in_specs=[pl.BlockSpec((1, n_heads, dh), lambda s: (s, 0, 0)),
                  pl.BlockSpec((1, nblk, n_heads, dh), lambda s: (s, 0, 0, 0))],
        out_specs=pl.BlockSpec((1, SUBLANES, n_heads, dh), lambda s: (s, 0, 0, 0)),
        out_shape=jax.ShapeDtypeStruct((b, SUBLANES, n_heads, dh), jnp.int32),
        compiler_params=_params(1),
        name="moba_sample_gate",
    )(q_s3, kmean)


def _sample_attn_body(pt_ref, top_ref, q_ref, kn_ref, vn_ref, ck_ref, cv_ref, o_ref, kbuf, vbuf, sem, *, n_heads):
    s = pl.program_id(0)
    ppb = MB_BLOCK // PAGE_SIZE
    n_pg = MB_TOPK * ppb
    dh = MB_HEAD_DIM

    def copies(seq, slot):
        out = []
        for h in range(n_heads):
            for t in range(MB_TOPK):
                blk_id = top_ref[seq, h * MB_TOPK + t]
                for e in range(ppb):
                    page = pt_ref[seq, ppb * blk_id + e]
                    out.append(pltpu.make_async_copy(ck_ref.at[0, page, :, h, :], kbuf.at[slot, h, t * ppb + e],
                                                     sem.at[0, slot]))
                    out.append(pltpu.make_async_copy(cv_ref.at[0, page, :, h, :], vbuf.at[slot, h, t * ppb + e],
                                                     sem.at[1, slot]))
        return out

    slot = s % 2

    @pl.when(s == 0)
    def _():
        for cp in copies(0, 0):
            cp.start()

    @pl.when(s + 1 < pl.num_programs(0))
    def _():
        for cp in copies(s + 1, 1 - slot):
            cp.start()

    for cp in copies(s, slot):
        cp.wait()

    scale = dh ** -0.5
    for h in range(n_heads):
        q = q_ref[0, h:h + 1, :] * scale
        keys = kbuf[slot, h].reshape(n_pg * PAGE_SIZE, dh).astype(BF16)
        vals = vbuf[slot, h].reshape(n_pg * PAGE_SIZE, dh).astype(BF16)
        sc = _dot_nt(jnp.broadcast_to(q, (SUBLANES, dh)).astype(BF16), keys)[0:1, :]
        s_self = jnp.sum(q * kn_ref[0, h:h + 1, :], axis=-1, keepdims=True)
        m = jnp.maximum(jnp.max(sc, axis=-1, keepdims=True), s_self)
        p = jnp.exp(sc - m)
        p_self = jnp.exp(s_self - m)
        l = jnp.sum(p, axis=-1, keepdims=True) + p_self
        pv = _dot(jnp.broadcast_to(p, (SUBLANES, p.shape[1])).astype(BF16), vals)[0:1, :]
        o_ref[0, h:h + 1, :] = (pv + vn_ref[0, h:h + 1, :] * p_self) * (1.0 / l)


def _sample_attn(q_s3, k_s3, v_s3, cache_k, cache_v, page_table, top, n_heads):
    b = q_s3.shape[0]
    dh = MB_HEAD_DIM
    n_pg = MB_TOPK * (MB_BLOCK // PAGE_SIZE)
    row = pl.BlockSpec((1, n_heads, dh), lambda s, pt, tp: (s, 0, 0))
    hbm = pl.BlockSpec(memory_space=pl.ANY)
    grid_spec = pltpu.PrefetchScalarGridSpec(
        num_scalar_prefetch=2,
        grid=(b,),
        in_specs=[row, row, row, hbm, hbm],
        out_specs=row,
        scratch_shapes=[pltpu.VMEM((2, n_heads, n_pg, PAGE_SIZE, dh), F32),
                        pltpu.VMEM((2, n_heads, n_pg, PAGE_SIZE, dh), F32),
                        pltpu.SemaphoreType.DMA((2, 2))],
    )
    return pl.pallas_call(
        functools.partial(_sample_attn_body, n_heads=n_heads),
        grid_spec=grid_spec,
        out_shape=jax.ShapeDtypeStruct((b, n_heads, dh), F32),
        compiler_params=_params(1),
        name="moba_sample_attn",
    )(page_table, top, q_s3, k_s3, v_s3, cache_k, cache_v)


def _layer(xp, xs, cache_k, cache_v, wkv_s, shift_s, page_table, norm_mix, w_in, rwkv_mu, w0, w2, a0, a2, g2,
           k_k, k_a, r_k, lnx_g, lnx_b, up_rw, up_mb, w_o, norm_ffn, w_gate, w_up, w_down):
    t, d_model = xp.shape
    b = xs.shape[0]
    n_rw_heads = r_k.shape[0]
    rw_width = n_rw_heads * RW_HEAD_DIM
    n_mb_heads = cache_k.shape[3]
    mb_width = n_mb_heads * MB_HEAD_DIM
    rw_proj = rwkv_mu.shape[0]
    lora = DECAY_LORA + ICLR_LORA + GATE_LORA
    assert rw_proj == 3 * rw_width + lora
    rw_pad = 3 * rw_width + LORA_TAIL
    rw_col0 = 3 * mb_width
    gate_col0 = rw_col0 + rw_proj
    d_ff = w_gate.shape[1]

    xn_p = _rmsnorm(xp, norm_mix, BF16)
    xn_s = _rmsnorm(xs, norm_mix, BF16)

    w_in_t = jnp.swapaxes(w_in, 0, 1)

    def proj(name, col0, n_out, ep=_ep_identity, dt=F32):
        tm, tn = TILE_WIDE
        (o_p,), (o_s,) = _fused_matmul(name, [xn_p], [xn_s], [w_in_t], [col0], n_out, _pick_tile(n_out, tn), tm,
                                       [], ep, [dt], w_t=True)
        return o_p, o_s

    q_p, q_s = proj("proj_q", 0, mb_width)
    k_p, k_s = proj("proj_k", mb_width, mb_width)
    v_p, v_s = proj("proj_v", 2 * mb_width, mb_width)
    p_rw_p, p_rw_s = proj("proj_rw", rw_col0, rw_pad)
    sg_p, sg_s = proj("proj_gates", gate_col0, 2 * d_model, ep=_ep_sigmoid, dt=BF16)

    pad_row = lambda a: jnp.pad(a, (0, rw_pad - rw_proj)).reshape(1, rw_pad)
    lora_pad = lambda a, r0: jnp.pad(a, ((r0, LORA_TAIL - r0 - a.shape[0]), (0, 0))).astype(BF16)
    prm = dict(
        mu=pad_row(rwkv_mu), w0=w0.reshape(1, rw_width), a0=a0.reshape(1, rw_width),
        k_k=k_k.reshape(1, rw_width), k_a=k_a.reshape(1, rw_width), r_k=r_k.reshape(1, rw_width),
        lnx_g=lnx_g.reshape(1, rw_width), lnx_b=lnx_b.reshape(1, rw_width),
        w2p=lora_pad(w2, 0), a2p=lora_pad(a2, DECAY_LORA), g2p=lora_pad(g2, DECAY_LORA + ICLR_LORA))
    y_rw_p, st_pairs = _rw_scan_prompt(p_rw_p, prm, rw_width)
    n_pair = rw_width // LANES
    st5 = st_pairs.reshape(n_pair, 2, RW_HEAD_DIM, 2, RW_HEAD_DIM)
    wkv_p = jnp.stack([st5[:, 0, :, 0, :], st5[:, 1, :, 1, :]], axis=1)
    wkv_p = jnp.swapaxes(wkv_p, -1, -2).reshape(n_rw_heads, RW_HEAD_DIM, RW_HEAD_DIM)
    shift_p = p_rw_p[t - 1, :rw_proj]

    prev_s = jnp.pad(shift_s, ((0, 0), (0, rw_pad - rw_proj)))
    rows = _rw_sample_prep(p_rw_s, prev_s, prm, rw_width)
    rows = [a.reshape(b, n_rw_heads, 1, RW_HEAD_DIM) for a in rows]
    head_prm = [a.reshape(n_rw_heads, 1, RW_HEAD_DIM) for a in (k_k, k_a, r_k, lnx_g, lnx_b)]
    wkv_s_new, y_rw_s = _rw_sample_step(wkv_s, rows, head_prm)
    y_rw_s = y_rw_s.reshape(b, rw_width).astype(BF16)
    shift_s_new = p_rw_s[:, :rw_proj]

    y_mb_p, kmean_s = _moba_prompt(q_p, k_p, v_p, n_mb_heads, cache_k, page_table)
    q_s3 = q_s.reshape(b, n_mb_heads, MB_HEAD_DIM)
    top = _sample_gate(q_s3, kmean_s)
    top = jnp.swapaxes(top[:, :MB_TOPK, :, 0], 1, 2).reshape(b, n_mb_heads * MB_TOPK)
    y_mb_s = _sample_attn(q_s3, k_s.reshape(b, n_mb_heads, MB_HEAD_DIM), v_s.reshape(b, n_mb_heads, MB_HEAD_DIM),
                          cache_k, cache_v, page_table, top, n_mb_heads)
    y_mb_s = y_mb_s.reshape(b, mb_width).astype(BF16)

    tm, tn = TILE_WIDE
    (mg_p,), (mg_s,) = _fused_matmul("merge", [y_rw_p, y_mb_p], [y_rw_s, y_mb_s], [up_rw, up_mb], [0, 0],
                                     d_model, _pick_tile(d_model, tn), tm,
                                     [(sg_p, sg_s, 0), (sg_p, sg_s, d_model)], _ep_merge, [BF16])
    (x2_p,), (x2_s,) = _fused_matmul("out_proj", [mg_p], [mg_s], [w_o], [0], d_model, _pick_tile(d_model, tn), tm,
                                     [(xp, xs, 0)], _ep_residual, [F32])
    xn2_p = _rmsnorm(x2_p, norm_ffn, BF16)
    xn2_s = _rmsnorm(x2_s, norm_ffn, BF16)
    tm, tn = TILE_TWO_WEIGHTS
    (h_p,), (h_s,) = _fused_matmul("ffn_up", [xn2_p], [xn2_s], [w_gate, w_up], [0, 0],
                                   d_ff, _pick_tile(d_ff, tn), tm, [], _ep_swiglu, [BF16], x_of=[0, 0])
    tm, tn = TILE_LONG_K
    (x3_p,), (x3_s,) = _fused_matmul("ffn_down", [h_p], [h_s], [w_down.astype(BF16)], [0], d_model,
                                     _pick_tile(d_model, tn), tm, [(x2_p, x2_s, 0)], _ep_residual, [F32])
    return (x3_p, x3_s, k_p, v_p, wkv_p, shift_p, k_s, v_s, wkv_s_new, shift_s_new)


def kernel(x_prompt, x_sample, cache_k, cache_v, state_wkv, state_shift, page_table, norm_mix, w_in, rwkv_mu, w0, w2, a0, a2, g2, k_k, k_a, r_k, lnx_g, lnx_b, up_rw, up_mb, w_o, norm_ffn, w_gate, w_up, w_down, norm_final):
    depth = w_in.shape[0]
    assert depth == 1 and x_prompt.shape[0] == 1 and x_sample.shape[1] == 1
    t, d_model = x_prompt.shape[1], x_prompt.shape[2]
    b = x_sample.shape[0]
    n_mb_heads = cache_k.shape[3]
    n_rw_heads = r_k.shape[1]
    out = _layer(x_prompt[0], x_sample[:, 0], cache_k, cache_v, state_wkv[0], state_shift[0], page_table,
                 norm_mix[0], w_in[0], rwkv_mu[0], w0[0], w2[0], a0[0], a2[0], g2[0], k_k[0], k_a[0], r_k[0],
                 lnx_g[0], lnx_b[0], up_rw[0], up_mb[0], w_o[0], norm_ffn[0], w_gate[0], w_up[0], w_down[0])
    x3_p, x3_s, k_p, v_p, wkv_p, shift_p, k_s, v_s, wkv_s, shift_s = out
    y_p = _rmsnorm(x3_p, norm_final, F32)
    y_s = _rmsnorm(x3_s, norm_final, F32)
    rw_proj = rwkv_mu.shape[1]
    return (y_p.reshape(1, t, d_model), y_s.reshape(b, 1, d_model),
            k_p.reshape(1, 1, t, n_mb_heads, MB_HEAD_DIM), v_p.reshape(1, 1, t, n_mb_heads, MB_HEAD_DIM),
            wkv_p.reshape(1, 1, n_rw_heads, RW_HEAD_DIM, RW_HEAD_DIM), shift_p.reshape(1, 1, rw_proj),
            k_s.reshape(1, b, 1, n_mb_heads, MB_HEAD_DIM), v_s.reshape(1, b, 1, n_mb_heads, MB_HEAD_DIM),
            wkv_s.reshape(1, b, n_rw_heads, RW_HEAD_DIM, RW_HEAD_DIM), shift_s.reshape(1, b, rw_proj))
```

```python
import functools

import jax
import jax.numpy as jnp
from jax import lax
from jax.experimental import pallas as pl
from jax.experimental.pallas import tpu as pltpu

F32 = jnp.float32
BF16 = jnp.bfloat16

RMS_EPS = 1e-6
LNX_EPS = 64e-5
MB_BLOCK = 256
MB_TOPK = 3
MB_HEAD_DIM = 128
PAGE_SIZE = 128
RW_HEAD_DIM = 64
DECAY_LORA = 96
ICLR_LORA = 96
GATE_LORA = 256
LORA_TAIL = 512
SCAN_CHUNK = 64
LANES = 128
SUBLANES = 8
VMEM_LIMIT = 56 * 1024 * 1024

TILE_WIDE = (1024, 512)
TILE_TWO_WEIGHTS = (1024, 256)
TILE_LONG_K = (512, 512)

_NEG_INF = float("-inf")
_LOG2E = 1.4426950408889634


def _params(n_axes):
    return pltpu.CompilerParams(dimension_semantics=("arbitrary",) * n_axes,
                                vmem_limit_bytes=VMEM_LIMIT)


def _dot(a, b):
    return jnp.dot(a, b, preferred_element_type=F32)


def _dot_nt(a, b):
    return lax.dot_general(a, b, (((1,), (1,)), ((), ())), preferred_element_type=F32)


def _dot_tn(a, b):
    return lax.dot_general(a, b, (((0,), (0,)), ((), ())), preferred_element_type=F32)


def _split2(x):
    hi = x.astype(BF16)
    lo = (x - hi.astype(F32)).astype(BF16)
    return hi, lo


def _split3(x):
    hi = x.astype(BF16)
    r1 = x - hi.astype(F32)
    mid = r1.astype(BF16)
    lo = (r1 - mid.astype(F32)).astype(BF16)
    return hi, mid, lo


def _sigmoid(x):
    return 1.0 / (1.0 + jnp.exp(-x))


def _softplus(x):
    return jnp.maximum(x, 0.0) + jnp.log1p(jnp.exp(-jnp.abs(x)))


def _rmsnorm_body(x_ref, g_ref, o_ref):
    x = x_ref[...]
    ms = jnp.mean(x * x, axis=-1, keepdims=True)
    o_ref[...] = (x * lax.rsqrt(ms + RMS_EPS) * g_ref[...]).astype(o_ref.dtype)


def _rmsnorm(x, g, out_dtype, tm=256):
    m, d = x.shape
    tm = min(tm, m)
    assert m % tm == 0
    return pl.pallas_call(
        _rmsnorm_body,
        grid=(m // tm,),
        in_specs=[pl.BlockSpec((tm, d), lambda i: (i, 0)), pl.BlockSpec((1, d), lambda i: (0, 0))],
        out_specs=pl.BlockSpec((tm, d), lambda i: (i, 0)),
        out_shape=jax.ShapeDtypeStruct((m, d), out_dtype),
        compiler_params=_params(1),
        name="rmsnorm",
    )(x, g.reshape(1, d))


def _pick_tile(n, pref):
    t = pref
    while n % t:
        t //= 2
    assert t >= LANES
    return t


def _fused_matmul(name, xp, xs, ws, w_col0, n_out, tn, tm, extras, epilogue, out_dtypes, x_of=None, w_t=False):
    nd, ne, no, nx = len(ws), len(extras), len(out_dtypes), len(xp)
    mm = _dot_nt if w_t else _dot
    x_of = list(range(nd)) if x_of is None else x_of
    mp, ms = xp[0].shape[0], xs[0].shape[0]
    tm = min(tm, mp)
    assert mp % tm == 0 and n_out % tn == 0
    nj, ni = n_out // tn, mp // tm
    cast = [w.dtype != BF16 for w in ws]
    n_cast = sum(cast)
    cast_slot = [sum(cast[:k]) for k in range(nd)]

    def body(*refs):
        xp_r = refs[:nx]
        xs_r = refs[nx:2 * nx]
        w_r = refs[2 * nx:2 * nx + nd]
        base = 2 * nx + nd
        ep_r = refs[base:base + ne]
        es_r = refs[base + ne:base + 2 * ne]
        base += 2 * ne
        op_r = refs[base:base + no]
        os_r = refs[base + no:base + 2 * no]
        wb_r = refs[base + 2 * no:]
        i = pl.program_id(1)

        def weight(k):
            return wb_r[cast_slot[k]][...] if cast[k] else w_r[k][...]

        @pl.when(i == 0)
        def _():
            for k in range(nd):
                if cast[k]:
                    wb_r[cast_slot[k]][...] = w_r[k][...].astype(BF16)
            accs = [mm(xs_r[x_of[k]][...], weight(k)) for k in range(nd)]
            outs = epilogue(accs, [e[...] for e in es_r])
            for o, val in zip(os_r, outs):
                o[...] = val.astype(o.dtype)

        accs = [mm(xp_r[x_of[k]][...], weight(k)) for k in range(nd)]
        outs = epilogue(accs, [e[...] for e in ep_r])
        for o, val in zip(op_r, outs):
            o[...] = val.astype(o.dtype)

    in_specs = []
    for k in range(nx):
        kk = xp[k].shape[1]
        in_specs.append(pl.BlockSpec((tm, kk), lambda j, i: (i, 0)))
    for k in range(nx):
        kk = xs[k].shape[1]
        in_specs.append(pl.BlockSpec((ms, kk), lambda j, i: (0, 0)))
    for k in range(nd):
        c0 = w_col0[k]
        if not w_t:
            assert c0 % tn == 0
            in_specs.append(pl.BlockSpec((ws[k].shape[0], tn), functools.partial(lambda j, i, c: (0, j + c), c=c0 // tn)))
        elif c0 % tn == 0:
            in_specs.append(pl.BlockSpec((tn, ws[k].shape[1]), functools.partial(lambda j, i, c: (j + c, 0), c=c0 // tn)))
        else:
            assert c0 % SUBLANES == 0
            in_specs.append(pl.BlockSpec((pl.Element(tn), pl.Element(ws[k].shape[1])),
                                         functools.partial(lambda j, i, c: (pl.multiple_of(c + j * tn, SUBLANES), 0), c=c0)))
    for (_, _, c0) in extras:
        assert c0 % tn == 0
        in_specs.append(pl.BlockSpec((tm, tn), functools.partial(lambda j, i, c: (i, j + c), c=c0 // tn)))
    for (_, _, c0) in extras:
        in_specs.append(pl.BlockSpec((ms, tn), functools.partial(lambda j, i, c: (0, j + c), c=c0 // tn)))
    out_specs = [pl.BlockSpec((tm, tn), lambda j, i: (i, j)) for _ in range(no)]
    out_specs += [pl.BlockSpec((ms, tn), lambda j, i: (0, j)) for _ in range(no)]
    out_shape = [jax.ShapeDtypeStruct((mp, n_out), dt) for dt in out_dtypes]
    out_shape += [jax.ShapeDtypeStruct((ms, n_out), dt) for dt in out_dtypes]
    scratch = [pltpu.VMEM((tn, ws[k].shape[1]) if w_t else (ws[k].shape[0], tn), BF16) for k in range(nd) if cast[k]]
    assert len(scratch) == n_cast
    args = list(xp) + list(xs) + list(ws) + [e[0] for e in extras] + [e[1] for e in extras]
    outs = pl.pallas_call(
        body,
        grid=(nj, ni),
        in_specs=in_specs,
        out_specs=out_specs,
        out_shape=out_shape,
        scratch_shapes=scratch,
        compiler_params=_params(2),
        name=name,
    )(*args)
    return outs[:no], outs[no:]


def _ep_identity(accs, extras):
    return (accs[0],)


def _ep_sigmoid(accs, extras):
    return (_sigmoid(accs[0]),)


def _ep_merge(accs, extras):
    return (extras[0].astype(F32) * accs[0] + extras[1].astype(F32) * accs[1],)


def _ep_residual(accs, extras):
    return (extras[0] + accs[0],)


def _ep_swiglu(accs, extras):
    g = accs[0]
    return (g * _sigmoid(g) * accs[1],)


def _rw_prep(p, prev, mu, w0, a0, w2p, a2p, g2p, width):
    pm = p + (prev - p) * mu
    r = pm[:, :width]
    k = pm[:, width:2 * width]
    v = pm[:, 2 * width:3 * width]
    tail = pm[:, 3 * width:3 * width + LORA_TAIL]
    lw = w0 + _dot(jnp.tanh(tail).astype(BF16), w2p)
    w_log = -_softplus(-lw) - 0.5
    logd = -jnp.exp(w_log)
    a_lr = _sigmoid(a0 + _dot(tail.astype(BF16), a2p))
    g = _dot(_sigmoid(tail).astype(BF16), g2p)
    return r, k, v, logd, a_lr, g


def _seg_ones():
    ri = lax.broadcasted_iota(jnp.int32, (LANES, LANES), 0) // RW_HEAD_DIM
    ci = lax.broadcasted_iota(jnp.int32, (LANES, LANES), 1) // RW_HEAD_DIM
    return jnp.where(ri == ci, 1.0, 0.0).astype(BF16)


def _seg_sum(x, ones_bd):
    return _dot(x.astype(BF16), ones_bd)


def _rw_scan_body(p_ref, mu_ref, w0_ref, a0_ref, kk_ref, ka_ref, rk_ref, lg_ref, lb_ref,
                  w2_ref, a2_ref, g2_ref, y_ref, st_ref, pbuf_ref, state_ref, *, width):
    c = pl.program_id(0)
    ch = SCAN_CHUNK
    n_pair = width // LANES

    @pl.when(c == 0)
    def _():
        pbuf_ref[...] = jnp.zeros_like(pbuf_ref)
        state_ref[...] = jnp.zeros_like(state_ref)

    pbuf_ref[pl.ds(SUBLANES, ch), :] = p_ref[...]
    p = pbuf_ref[pl.ds(SUBLANES, ch), :]
    prev = pbuf_ref[pl.ds(SUBLANES - 1, ch), :]
    r, k, v, logd, a_lr, g = _rw_prep(p, prev, mu_ref[...], w0_ref[...], a0_ref[...],
                                      w2_ref[...], a2_ref[...], g2_ref[...], width)
    pbuf_ref[pl.ds(0, SUBLANES), :] = pbuf_ref[pl.ds(ch, SUBLANES), :]

    ti = lax.broadcasted_iota(jnp.int32, (ch, ch), 0)
    tj = lax.broadcasted_iota(jnp.int32, (ch, ch), 1)
    ltri = jnp.where(ti >= tj, 1.0, 0.0).astype(BF16)
    d_hi, d_mid, d_lo = _split3(logd)
    lc = _dot(ltri, d_hi) + _dot(ltri, d_mid) + _dot(ltri, d_lo)
    lc_end = lc[ch - 1:ch, :]
    w_t = jnp.exp(lc)
    w_tm1 = jnp.exp(lc - logd)
    w_inv = jnp.exp(-lc)
    w_rem = jnp.exp(lc_end - lc)
    w_end = jnp.exp(lc_end)

    ones_bd = _seg_ones()
    kkr = k * kk_ref[...]
    kf = k * (1.0 + (a_lr - 1.0) * ka_ref[...])
    rkk = r * kf * rk_ref[...]

    lane = lax.broadcasted_iota(jnp.int32, (ch, LANES), 1)
    m0 = jnp.where(lane < RW_HEAD_DIM, 1.0, 0.0)
    m1 = 1.0 - m0

    def stack(z):
        return jnp.concatenate([z * m0, z * m1], axis=0)

    ri = lax.broadcasted_iota(jnp.int32, (2 * ch, 2 * ch), 0) % ch
    ci = lax.broadcasted_iota(jnp.int32, (2 * ch, 2 * ch), 1) % ch
    tri_strict = jnp.where(ri > ci, 1.0, 0.0)
    tri_incl = jnp.where(ri >= ci, 1.0, 0.0)
    er = lax.broadcasted_iota(jnp.int32, (LANES, LANES), 0)
    ec = lax.broadcasted_iota(jnp.int32, (LANES, LANES), 1)
    eye = jnp.where(er == ec, 1.0, 0.0)

    pairs = range(n_pair)
    sls = [slice(gp * LANES, (gp + 1) * LANES) for gp in pairs]
    h2 = 2 * ch

    def rows_of(z_list):
        return jnp.concatenate(z_list, axis=0)

    kkr_g = [kkr[:, sl] for sl in sls]
    ss = _seg_sum(rows_of([z * z for z in kkr_g]), ones_bd)
    kk_g = [kkr_g[gp] / jnp.maximum(jnp.sqrt(ss[gp * ch:(gp + 1) * ch]), 1e-12) for gp in pairs]
    braw_g = [kk_g[gp] * a_lr[:, sls[gp]] for gp in pairs]
    kf_g = [kf[:, sl] for sl in sls]
    a_s = [stack(-kk_g[gp] * w_tm1[:, sls[gp]]).astype(BF16) for gp in pairs]
    r_s = [stack(r[:, sls[gp]] * w_t[:, sls[gp]]).astype(BF16) for gp in pairs]
    v_s = [stack(v[:, sl]).astype(BF16) for sl in sls]
    rhs = [jnp.concatenate([stack(braw_g[gp] * w_inv[:, sls[gp]]), stack(kf_g[gp] * w_inv[:, sls[gp]])],
                           axis=0).astype(BF16) for gp in pairs]
    upd_l = [jnp.concatenate([stack(braw_g[gp] * w_rem[:, sls[gp]]), stack(kf_g[gp] * w_rem[:, sls[gp]])],
                             axis=0).astype(BF16) for gp in pairs]
    gram = [_dot_nt(jnp.concatenate([a_s[gp], r_s[gp]], axis=0), rhs[gp]) for gp in pairs]
    pw = [(gram[gp][:h2, :h2] * tri_strict).astype(BF16) for gp in pairs]
    n_bd = [(gram[gp][:h2, h2:] * tri_strict).astype(BF16) for gp in pairs]
    pb = [(gram[gp][h2:, :h2] * tri_incl).astype(BF16) for gp in pairs]
    pk = [(gram[gp][h2:, h2:] * tri_incl).astype(BF16) for gp in pairs]

    st = [state_ref[gp] for gp in pairs]
    st_b = [z.astype(BF16) for z in st]
    sa = [_dot(jnp.concatenate([a_s[gp], n_bd[gp]], axis=1), jnp.concatenate([st_b[gp], v_s[gp]], axis=0))
          for gp in pairs]
    n_steps = ch.bit_length() - 1
    for s in range(n_steps):
        sa = [sa[gp] + _dot(pw[gp], sa[gp].astype(BF16)) for gp in pairs]
        if s + 1 < n_steps:
            pw = [_dot(pw[gp], pw[gp]).astype(BF16) for gp in pairs]
    sa_b = [z.astype(BF16) for z in sa]
    y_s = [_dot(jnp.concatenate([r_s[gp], pb[gp], pk[gp]], axis=1),
                jnp.concatenate([st_b[gp], sa_b[gp], v_s[gp]], axis=0)) for gp in pairs]
    upd = [_dot_tn(upd_l[gp], jnp.concatenate([sa_b[gp], v_s[gp]], axis=0)) for gp in pairs]
    for gp in pairs:
        w_col = jnp.sum(eye * w_end[:, sls[gp]], axis=1, keepdims=True)
        state_ref[gp] = st[gp] * w_col + upd[gp]

    y = rows_of([y_s[gp][:ch] + y_s[gp][ch:] for gp in pairs])
    mean = _seg_sum(y, ones_bd) * (1.0 / RW_HEAD_DIM)
    d = y - mean
    var = _seg_sum(d * d, ones_bd) * (1.0 / RW_HEAD_DIM)
    yn = d * lax.rsqrt(var + LNX_EPS)
    bonus = _seg_sum(rows_of([rkk[:, sl] for sl in sls]), ones_bd)
    for gp in pairs:
        sl = sls[gp]
        rs = slice(gp * ch, (gp + 1) * ch)
        out = (yn[rs] * lg_ref[:, sl] + lb_ref[:, sl] + bonus[rs] * v[:, sl]) * g[:, sl]
        y_ref[:, sl] = out.astype(y_ref.dtype)

    @pl.when(c == pl.num_programs(0) - 1)
    def _():
        st_ref[...] = state_ref[...]


def _rw_scan_prompt(p_rw, prm, width):
    t, pw = p_rw.shape
    ch = SCAN_CHUNK
    assert t % ch == 0 and width % LANES == 0
    n_pair = width // LANES
    row = lambda n: pl.BlockSpec((1, n), lambda c: (0, 0))
    full = lambda a: pl.BlockSpec(a.shape, lambda c: (0, 0))
    return pl.pallas_call(
        functools.partial(_rw_scan_body, width=width),
        grid=(t // ch,),
        in_specs=[pl.BlockSpec((ch, pw), lambda c: (c, 0)), row(pw)] + [row(width)] * 7
                 + [full(prm["w2p"]), full(prm["a2p"]), full(prm["g2p"])],
        out_specs=[pl.BlockSpec((ch, width), lambda c: (c, 0)),
                   pl.BlockSpec((n_pair, LANES, LANES), lambda c: (0, 0, 0))],
        out_shape=[jax.ShapeDtypeStruct((t, width), BF16),
                   jax.ShapeDtypeStruct((n_pair, LANES, LANES), F32)],
        scratch_shapes=[pltpu.VMEM((ch + SUBLANES, pw), F32), pltpu.VMEM((n_pair, LANES, LANES), F32)],
        compiler_params=_params(1),
        name="rwkv_scan",
    )(p_rw, prm["mu"], prm["w0"], prm["a0"], prm["k_k"], prm["k_a"], prm["r_k"], prm["lnx_g"], prm["lnx_b"],
      prm["w2p"], prm["a2p"], prm["g2p"])


def _rw_sample_prep_body(p_ref, prev_ref, mu_ref, w0_ref, a0_ref, w2_ref, a2_ref, g2_ref,
                         r_ref, k_ref, v_ref, d_ref, alr_ref, g_ref, *, width):
    r, k, v, logd, a_lr, g = _rw_prep(p_ref[...], prev_ref[...], mu_ref[...], w0_ref[...], a0_ref[...],
                                      w2_ref[...], a2_ref[...], g2_ref[...], width)
    r_ref[...] = r
    k_ref[...] = k
    v_ref[...] = v
    d_ref[...] = logd
    alr_ref[...] = a_lr
    g_ref[...] = g


def _rw_sample_prep(p_s, prev_s, prm, width):
    b = p_s.shape[0]
    args = (p_s, prev_s, prm["mu"], prm["w0"], prm["a0"], prm["w2p"], prm["a2p"], prm["g2p"])
    return pl.pallas_call(
        functools.partial(_rw_sample_prep_body, width=width),
        grid=(1,),
        in_specs=[pl.BlockSpec(a.shape, lambda i: (0, 0)) for a in args],
        out_specs=[pl.BlockSpec((b, width), lambda i: (0, 0))] * 6,
        out_shape=[jax.ShapeDtypeStruct((b, width), F32)] * 6,
        compiler_params=_params(1),
        name="rwkv_sample_prep",
    )(*args)


def _rw_sample_step_body(s_ref, r_ref, k_ref, v_ref, d_ref, alr_ref, g_ref,
                         kk_ref, ka_ref, rk_ref, lg_ref, lb_ref, so_ref, y_ref):
    s = s_ref[0]
    r, k, v = r_ref[0], k_ref[0], v_ref[0]
    a_lr, g = alr_ref[0], g_ref[0]
    w = jnp.exp(d_ref[0])
    kkr = k * kk_ref[...]
    kk = kkr / jnp.maximum(jnp.sqrt(jnp.sum(kkr * kkr, axis=-1, keepdims=True)), 1e-12)
    kf = k * (1.0 + (a_lr - 1.0) * ka_ref[...])
    a = -kk
    b = kk * a_lr
    n = RW_HEAD_DIM
    eye = (lax.broadcasted_iota(jnp.int32, (n, n), 0) == lax.broadcasted_iota(jnp.int32, (n, n), 1))
    sa = jnp.sum(s * a, axis=-1, keepdims=True)
    v_col = jnp.sum(jnp.where(eye, v, 0.0), axis=-1, keepdims=True)
    s1 = s * w + sa * b + v_col * kf
    so_ref[0] = s1
    y_col = jnp.sum(s1 * r, axis=-1, keepdims=True)
    y = jnp.sum(jnp.where(eye, y_col, 0.0), axis=-2, keepdims=True)
    mean = jnp.mean(y, axis=-1, keepdims=True)
    d = y - mean
    var = jnp.mean(d * d, axis=-1, keepdims=True)
    yn = d * lax.rsqrt(var + LNX_EPS) * lg_ref[...] + lb_ref[...]
    bonus = jnp.sum(r * kf * rk_ref[...], axis=-1, keepdims=True) * v
    y_ref[0] = (yn + bonus) * g


def _rw_sample_step(state, rows, head_prm):
    b, h, n, _ = state.shape
    vec = pl.BlockSpec((1, h, 1, n), lambda i: (i, 0, 0, 0))
    par = pl.BlockSpec((h, 1, n), lambda i: (0, 0, 0))
    mat = pl.BlockSpec((1, h, n, n), lambda i: (i, 0, 0, 0))
    return pl.pallas_call(
        _rw_sample_step_body,
        grid=(b,),
        in_specs=[mat] + [vec] * 6 + [par] * 5,
        out_specs=[mat, vec],
        out_shape=[jax.ShapeDtypeStruct((b, h, n, n), F32), jax.ShapeDtypeStruct((b, h, 1, n), F32)],
        compiler_params=_params(1),
        name="rwkv_sample_step",
    )(state, *rows, *head_prm)


def _top_blocks(gs, n_valid):
    nb = gs.shape[0]
    rowi = lax.broadcasted_iota(jnp.int32, gs.shape, 0)
    rowf = rowi.astype(F32)
    gm = jnp.where(rowi < n_valid, gs, _NEG_INF)
    picked = jnp.zeros(gs.shape, F32)
    order = []
    for _ in range(MB_TOPK):
        mx = jnp.max(gm, axis=0, keepdims=True)
        first = jnp.min(jnp.where(gm == mx, rowf, float(nb)), axis=0, keepdims=True)
        hit = jnp.where(rowf == first, jnp.where(mx > _NEG_INF, 1.0, 0.0), 0.0)
        picked = picked + hit
        gm = jnp.where(hit > 0.0, _NEG_INF, gm)
        order.append(first)
    return picked, order


def _moba_prompt_body(pt_ref, q_ref, k_ref, v_ref, *refs, nb, group, hpb, pages):
    page_refs = refs[:pages]
    o_ref, pm_ref, kb_ref, vt_ref, vt1_ref, km_ref, bias_ref = refs[pages:]
    i = pl.program_id(1)
    blk, dh = MB_BLOCK, MB_HEAD_DIM
    heads = range(hpb)

    if pages:
        ppb = MB_BLOCK // PAGE_SIZE
        step = pl.program_id(0) * pl.num_programs(1) + i
        first = (step % (pm_ref.shape[1] * ppb // pages)) * (pages // ppb)
        for u in range(pages // ppb):
            tot = jnp.sum(page_refs[u * ppb][0, 0], axis=0)
            for e in range(1, ppb):
                tot = tot + jnp.sum(page_refs[u * ppb + e][0, 0], axis=0)
            pm_ref[0, first + u] = tot * (1.0 / MB_BLOCK)

    @pl.when(i == 0)
    def _():
        for e in heads:
            for n in range(nb):
                kn = k_ref[n * blk:(n + 1) * blk, e * dh:(e + 1) * dh]
                u = n % group
                kb_ref[e, n // group, u * blk:(u + 1) * blk, :] = kn.astype(BF16)
                km_ref[e, n:n + 1, :] = jnp.mean(kn, axis=0, keepdims=True)
                vn_t = v_ref[n * blk:(n + 1) * blk, e * dh:(e + 1) * dh].T.astype(BF16)
                vt_ref[e, n // group, :, u * blk:(u + 1) * blk] = vn_t
                vt1_ref[e, n] = vn_t

    q_t = [q_ref[:, e * dh:(e + 1) * dh].T for e in heads]
    for e in heads:
        q_hi, q_lo = _split2(q_t[e])
        k_hi, k_lo = _split2(km_ref[e])
        gs = _dot(k_hi, q_hi) + _dot(k_hi, q_lo) + _dot(k_lo, q_hi)
        picked, _ = _top_blocks(gs, i)
        bias_ref[e] = jnp.where(picked > 0.0, 0.0, _NEG_INF)

    q_s = [(q_t[e] * (dh ** -0.5 * _LOG2E)).astype(BF16) for e in heads]
    own = pl.multiple_of((i % group) * blk, blk)
    kr = lax.broadcasted_iota(jnp.int32, (blk, blk), 0)
    qc = lax.broadcasted_iota(jnp.int32, (blk, blk), 1)
    s = [jnp.where(kr <= qc, _dot(kb_ref[e, i // group, pl.ds(own, blk), :], q_s[e]), _NEG_INF)
         for e in heads]
    m = [jnp.max(s[e], axis=0, keepdims=True) for e in heads]
    p = [jnp.exp2(s[e] - m[e]) for e in heads]
    l = [jnp.sum(p[e], axis=0, keepdims=True) for e in heads]
    acc = [_dot(vt1_ref[e, i], p[e].astype(BF16)) for e in heads]

    def step(jj, carry):
        m, l, acc = carry
        s = [_dot(kb_ref[e, jj], q_s[e]) for e in heads]
        m_run = list(m)
        m_at, p_sum, p_bf = [[] for _ in heads], [[] for _ in heads], [[] for _ in heads]
        for u in range(group):
            for e in heads:
                t_u = s[e][u * blk:(u + 1) * blk] + bias_ref[e, pl.ds(jj * group + u, 1), :]
                m_run[e] = jnp.maximum(m_run[e], jnp.max(t_u, axis=0, keepdims=True))
                p_u = jnp.exp2(t_u - m_run[e])
                m_at[e].append(m_run[e])
                p_sum[e].append(jnp.sum(p_u, axis=0, keepdims=True))
                p_bf[e].append(p_u.astype(BF16))
        l_new, acc_new = [], []
        for e in heads:
            alpha = jnp.exp2(m[e] - m_run[e])
            tot = alpha * l[e]
            out = alpha * acc[e]
            for u in range(group):
                c_u = jnp.exp2(m_at[e][u] - m_run[e])
                tot = tot + c_u * p_sum[e][u]
                out = out + c_u * _dot(vt_ref[e, jj, :, u * blk:(u + 1) * blk], p_bf[e][u])
            l_new.append(tot)
            acc_new.append(out)
        return tuple(m_run), tuple(l_new), tuple(acc_new)

    m, l, acc = lax.fori_loop(0, (i + group - 1) // group, step, (tuple(m), tuple(l), tuple(acc)))
    for e in heads:
        o_ref[:, e * dh:(e + 1) * dh] = (acc[e] * (1.0 / l[e])).T.astype(o_ref.dtype)


def _moba_prompt(q, k, v, n_heads, cache_k, page_table):
    t = q.shape[0]
    blk, dh = MB_BLOCK, MB_HEAD_DIM
    assert t % blk == 0
    nb = t // blk
    group = 4 if nb % 4 == 0 else 1
    hpb = 2 if n_heads % 2 == 0 else 1
    n_steps = (n_heads // hpb) * nb
    b, n_pages = page_table.shape
    _, _, page, c_heads, c_dh = cache_k.shape
    ppb = MB_BLOCK // PAGE_SIZE
    pages = (b * n_pages) // n_steps
    assert page == PAGE_SIZE and pages * n_steps == b * n_pages and pages % ppb == 0 and n_pages % pages == 0
    per_seq = n_pages // pages

    def page_map(e):
        return lambda h, i, pt: (0, pt[(h * nb + i) // per_seq, ((h * nb + i) % per_seq) * pages + e], 0, 0, 0)

    held = pl.BlockSpec((t, hpb * dh), lambda h, i, pt: (0, h), pipeline_mode=pl.Buffered(1))
    grid_spec = pltpu.PrefetchScalarGridSpec(
        num_scalar_prefetch=1,
        grid=(n_heads // hpb, nb),
        in_specs=[pl.BlockSpec((blk, hpb * dh), lambda h, i, pt: (i, h)), held, held]
                 + [pl.BlockSpec((1, 1, page, c_heads, c_dh), page_map(e)) for e in range(pages)],
        out_specs=[pl.BlockSpec((blk, hpb * dh), lambda h, i, pt: (i, h)),
                   pl.BlockSpec((1, n_pages // ppb, c_heads, c_dh), lambda h, i, pt: ((h * nb + i) // per_seq, 0, 0, 0))],
        scratch_shapes=[pltpu.VMEM((hpb, nb // group, group * blk, dh), BF16),
                        pltpu.VMEM((hpb, nb // group, dh, group * blk), BF16),
                        pltpu.VMEM((hpb, nb, dh, blk), BF16),
                        pltpu.VMEM((hpb, nb, dh), F32), pltpu.VMEM((hpb, nb, blk), F32)],
    )
    return pl.pallas_call(
        functools.partial(_moba_prompt_body, nb=nb, group=group, hpb=hpb, pages=pages),
        grid_spec=grid_spec,
        out_shape=[jax.ShapeDtypeStruct((t, n_heads * dh), BF16),
                   jax.ShapeDtypeStruct((b, n_pages // ppb, c_heads, c_dh), F32)],
        compiler_params=_params(2),
        name="moba_prompt",
    )(page_table, q, k, v, *([cache_k] * pages))


def _sample_gate_body(q_ref, km_ref, o_ref):
    km = km_ref[0]
    nblk, n_heads, dh = km.shape
    prod = (km * q_ref[0]).reshape(nblk * n_heads, dh)
    ones = jnp.ones((dh, dh), BF16)
    hi, mid, lo = _split3(prod)
    gs = (_dot(hi, ones) + _dot(mid, ones) + _dot(lo, ones)).reshape(nblk, n_heads, dh)
    _, order = _top_blocks(gs, nblk)
    order.append(jnp.zeros((SUBLANES - MB_TOPK, n_heads, dh), F32))
    o_ref[0] = jnp.concatenate(order, axis=0).astype(jnp.int32)


def _sample_gate(q_s3, kmean):
    b, nblk, n_heads, dh = kmean.shape
    return pl.pallas_call(
        _sample_gate_body,
        grid=(b,),
        in_specs=[pl.BlockSpec((1, n_heads, dh), lambda s: (s, 0, 0)),
                  pl.BlockSpec((1, nblk, n_heads, dh), lambda s: (s, 0, 0, 0))],
        out_specs=pl.BlockSpec((1, SUBLANES, n_heads, dh), lambda s: (s, 0, 0, 0)),
        out_shape=jax.ShapeDtypeStruct((b, SUBLANES, n_heads, dh), jnp.int32),
        compiler_params=_params(1),
        name="moba_sample_gate",
    )(q_s3, kmean)


def _sample_attn_body(pt_ref, top_ref, q_ref, kn_ref, vn_ref, ck_ref, cv_ref, o_ref, kbuf, vbuf, sem, *, n_heads):
    s = pl.program_id(0)
    ppb = MB_BLOCK // PAGE_SIZE
    n_pg = MB_TOPK * ppb
    dh = MB_HEAD_DIM

    def copies(seq, slot):
        out = []
        for h in range(n_heads):
            for t in range(MB_TOPK):
                blk_id = top_ref[seq, h * MB_TOPK + t]
                for e in range(ppb):
                    page = pt_ref[seq, ppb * blk_id + e]
                    out.append(pltpu.make_async_copy(ck_ref.at[0, page, :, h, :], kbuf.at[slot, h, t * ppb + e],
                                                     sem.at[0, slot]))
                    out.append(pltpu.make_async_copy(cv_ref.at[0, page, :, h, :], vbuf.at[slot, h, t * ppb + e],
                                                     sem.at[1, slot]))
        return out

    slot = s % 2

    @pl.when(s == 0)
    def _():
        for cp in copies(0, 0):
            cp.start()

    @pl.when(s + 1 < pl.num_programs(0))
    def _():
        for cp in copies(s + 1, 1 - slot):
            cp.start()

    for cp in copies(s, slot):
        cp.wait()

    scale = dh ** -0.5
    heads = range(n_heads)
    n_keys = n_pg * PAGE_SIZE
    q = [q_ref[0, h:h + 1, :] * scale for h in heads]
    sc = [_dot_nt(jnp.broadcast_to(q[h], (SUBLANES, dh)).astype(BF16),
                  kbuf[slot, h].reshape(n_keys, dh).astype(BF16))[0:1, :] for h in heads]
    s_self = [jnp.sum(q[h] * kn_ref[0, h:h + 1, :], axis=-1, keepdims=True) for h in heads]
    m = [jnp.maximum(jnp.max(sc[h], axis=-1, keepdims=True), s_self[h]) for h in heads]
    p = [jnp.exp(sc[h] - m[h]) for h in heads]
    p_self = [jnp.exp(s_self[h] - m[h]) for h in heads]
    l = [jnp.sum(p[h], axis=-1, keepdims=True) + p_self[h] for h in heads]
    pv = [_dot(jnp.broadcast_to(p[h], (SUBLANES, n_keys)).astype(BF16),
               vbuf[slot, h].reshape(n_keys, dh).astype(BF16))[0:1, :] for h in heads]
    for h in heads:
        o_ref[0, h:h + 1, :] = (pv[h] + vn_ref[0, h:h + 1, :] * p_self[h]) * (1.0 / l[h])


def _sample_attn(q_s3, k_s3, v_s3, cache_k, cache_v, page_table, top, n_heads):
    b = q_s3.shape[0]
    dh = MB_HEAD_DIM
    n_pg = MB_TOPK * (MB_BLOCK // PAGE_SIZE)
    row = pl.BlockSpec((1, n_heads, dh), lambda s, pt, tp: (s, 0, 0))
    hbm = pl.BlockSpec(memory_space=pl.ANY)
    grid_spec = pltpu.PrefetchScalarGridSpec(
        num_scalar_prefetch=2,
        grid=(b,),
        in_specs=[row, row, row, hbm, hbm],
        out_specs=row,
        scratch_shapes=[pltpu.VMEM((2, n_heads, n_pg, PAGE_SIZE, dh), F32),
                        pltpu.VMEM((2, n_heads, n_pg, PAGE_SIZE, dh), F32),
                        pltpu.SemaphoreType.DMA((2, 2))],
    )
    return pl.pallas_call(
        functools.partial(_sample_attn_body, n_heads=n_heads),
        grid_spec=grid_spec,
        out_shape=jax.ShapeDtypeStruct((b, n_heads, dh), F32),
        compiler_params=_params(1),
        name="moba_sample_attn",
    )(page_table, top, q_s3, k_s3, v_s3, cache_k, cache_v)


def _layer(xp, xs, cache_k, cache_v, wkv_s, shift_s, page_table, norm_mix, w_in, rwkv_mu, w0, w2, a0, a2, g2,
           k_k, k_a, r_k, lnx_g, lnx_b, up_rw, up_mb, w_o, norm_ffn, w_gate, w_up, w_down):
    t, d_model = xp.shape
    b = xs.shape[0]
    n_rw_heads = r_k.shape[0]
    rw_width = n_rw_heads * RW_HEAD_DIM
    n_mb_heads = cache_k.shape[3]
    mb_width = n_mb_heads * MB_HEAD_DIM
    rw_proj = rwkv_mu.shape[0]
    lora = DECAY_LORA + ICLR_LORA + GATE_LORA
    assert rw_proj == 3 * rw_width + lora
    rw_pad = 3 * rw_width + LORA_TAIL
    rw_col0 = 3 * mb_width
    gate_col0 = rw_col0 + rw_proj
    d_ff = w_gate.shape[1]

    xn_p = _rmsnorm(xp, norm_mix, BF16)
    xn_s = _rmsnorm(xs, norm_mix, BF16)

    w_in_t = jnp.swapaxes(w_in, 0, 1)

    def proj(name, col0, n_out, ep=_ep_identity, dt=F32):
        tm, tn = TILE_WIDE
        (o_p,), (o_s,) = _fused_matmul(name, [xn_p], [xn_s], [w_in_t], [col0], n_out, _pick_tile(n_out, tn), tm,
                                       [], ep, [dt], w_t=True)
        return o_p, o_s

    q_p, q_s = proj("proj_q", 0, mb_width)
    k_p, k_s = proj("proj_k", mb_width, mb_width)
    v_p, v_s = proj("proj_v", 2 * mb_width, mb_width)
    p_rw_p, p_rw_s = proj("proj_rw", rw_col0, rw_pad)
    sg_p, sg_s = proj("proj_gates", gate_col0, 2 * d_model, ep=_ep_sigmoid, dt=BF16)

    pad_row = lambda a: jnp.pad(a, (0, rw_pad - rw_proj)).reshape(1, rw_pad)
    lora_pad = lambda a, r0: jnp.pad(a, ((r0, LORA_TAIL - r0 - a.shape[0]), (0, 0))).astype(BF16)
    prm = dict(
        mu=pad_row(rwkv_mu), w0=w0.reshape(1, rw_width), a0=a0.reshape(1, rw_width),
        k_k=k_k.reshape(1, rw_width), k_a=k_a.reshape(1, rw_width), r_k=r_k.reshape(1, rw_width),
        lnx_g=lnx_g.reshape(1, rw_width), lnx_b=lnx_b.reshape(1, rw_width),
        w2p=lora_pad(w2, 0), a2p=lora_pad(a2, DECAY_LORA), g2p=lora_pad(g2, DECAY_LORA + ICLR_LORA))
    y_rw_p, st_pairs = _rw_scan_prompt(p_rw_p, prm, rw_width)
    n_pair = rw_width // LANES
    st5 = st_pairs.reshape(n_pair, 2, RW_HEAD_DIM, 2, RW_HEAD_DIM)
    wkv_p = jnp.stack([st5[:, 0, :, 0, :], st5[:, 1, :, 1, :]], axis=1)
    wkv_p = jnp.swapaxes(wkv_p, -1, -2).reshape(n_rw_heads, RW_HEAD_DIM, RW_HEAD_DIM)
    shift_p = p_rw_p[t - 1, :rw_proj]

    prev_s = jnp.pad(shift_s, ((0, 0), (0, rw_pad - rw_proj)))
    rows = _rw_sample_prep(p_rw_s, prev_s, prm, rw_width)
    rows = [a.reshape(b, n_rw_heads, 1, RW_HEAD_DIM) for a in rows]
    head_prm = [a.reshape(n_rw_heads, 1, RW_HEAD_DIM) for a in (k_k, k_a, r_k, lnx_g, lnx_b)]
    wkv_s_new, y_rw_s = _rw_sample_step(wkv_s, rows, head_prm)
    y_rw_s = y_rw_s.reshape(b, rw_width).astype(BF16)
    shift_s_new = p_rw_s[:, :rw_proj]

    y_mb_p, kmean_s = _moba_prompt(q_p, k_p, v_p, n_mb_heads, cache_k, page_table)
    q_s3 = q_s.reshape(b, n_mb_heads, MB_HEAD_DIM)
    top = _sample_gate(q_s3, kmean_s)
    top = jnp.swapaxes(top[:, :MB_TOPK, :, 0], 1, 2).reshape(b, n_mb_heads * MB_TOPK)
    y_mb_s = _sample_attn(q_s3, k_s.reshape(b, n_mb_heads, MB_HEAD_DIM), v_s.reshape(b, n_mb_heads, MB_HEAD_DIM),
                          cache_k, cache_v, page_table, top, n_mb_heads)
    y_mb_s = y_mb_s.reshape(b, mb_width).astype(BF16)

    tm, tn = TILE_WIDE
    (mg_p,), (mg_s,) = _fused_matmul("merge", [y_rw_p, y_mb_p], [y_rw_s, y_mb_s], [up_rw, up_mb], [0, 0],
                                     d_model, _pick_tile(d_model, tn), tm,
                                     [(sg_p, sg_s, 0), (sg_p, sg_s, d_model)], _ep_merge, [BF16])
    (x2_p,), (x2_s,) = _fused_matmul("out_proj", [mg_p], [mg_s], [w_o], [0], d_model, _pick_tile(d_model, tn), tm,
                                     [(xp, xs, 0)], _ep_residual, [F32])
    xn2_p = _rmsnorm(x2_p, norm_ffn, BF16)
    xn2_s = _rmsnorm(x2_s, norm_ffn, BF16)
    tm, tn = TILE_TWO_WEIGHTS
    (h_p,), (h_s,) = _fused_matmul("ffn_up", [xn2_p], [xn2_s], [w_gate, w_up], [0, 0],
                                   d_ff, _pick_tile(d_ff, tn), tm, [], _ep_swiglu, [BF16], x_of=[0, 0])
    tm, tn = TILE_LONG_K
    (x3_p,), (x3_s,) = _fused_matmul("ffn_down", [h_p], [h_s], [w_down.astype(BF16)], [0], d_model,
                                     _pick_tile(d_model, tn), tm, [(x2_p, x2_s, 0)], _ep_residual, [F32])
    return (x3_p, x3_s, k_p, v_p, wkv_p, shift_p, k_s, v_s, wkv_s_new, shift_s_new)


def kernel(x_prompt, x_sample, cache_k, cache_v, state_wkv, state_shift, page_table, norm_mix, w_in, rwkv_mu, w0, w2, a0, a2, g2, k_k, k_a, r_k, lnx_g, lnx_b, up_rw, up_mb, w_o, norm_ffn, w_gate, w_up, w_down, norm_final):
    depth = w_in.shape[0]
    assert depth == 1 and x_prompt.shape[0] == 1 and x_sample.shape[1] == 1
    t, d_model = x_prompt.shape[1], x_prompt.shape[2]
    b = x_sample.shape[0]
    n_mb_heads = cache_k.shape[3]
    n_rw_heads = r_k.shape[1]
    out = _layer(x_prompt[0], x_sample[:, 0], cache_k, cache_v, state_wkv[0], state_shift[0], page_table,
                 norm_mix[0], w_in[0], rwkv_mu[0], w0[0], w2[0], a0[0], a2[0], g2[0], k_k[0], k_a[0], r_k[0],
                 lnx_g[0], lnx_b[0], up_rw[0], up_mb[0], w_o[0], norm_ffn[0], w_gate[0], w_up[0], w_down[0])
    x3_p, x3_s, k_p, v_p, wkv_p, shift_p, k_s, v_s, wkv_s, shift_s = out
    y_p = _rmsnorm(x3_p, norm_final, F32)
    y_s = _rmsnorm(x3_s, norm_final, F32)
    rw_proj = rwkv_mu.shape[1]
    return (y_p.reshape(1, t, d_model), y_s.reshape(b, 1, d_model),
            k_p.reshape(1, 1, t, n_mb_heads, MB_HEAD_DIM), v_p.reshape(1, 1, t, n_mb_heads, MB_HEAD_DIM),
            wkv_p.reshape(1, 1, n_rw_heads, RW_HEAD_DIM, RW_HEAD_DIM), shift_p.reshape(1, 1, rw_proj),
            k_s.reshape(1, b, 1, n_mb_heads, MB_HEAD_DIM), v_s.reshape(1, b, 1, n_mb_heads, MB_HEAD_DIM),
            wkv_s.reshape(1, b, n_rw_heads, RW_HEAD_DIM, RW_HEAD_DIM), shift_s.reshape(1, b, rw_proj))
```
